```python
import math
import jax, jax.numpy as jnp
from jax import lax
import numpy as np


D_MODEL = 1024
BATCH = 8
SEQ = 4096
DEPTH = 4

N_A = DEPTH // 2
N_B = DEPTH - N_A
N_HEADS = 16
HEAD_DIM = D_MODEL // N_HEADS
CONV_W = 31
FFN_DIM = 2816
FFN_CONV_W = 3
PLE_DIM = 256
Q_BLOCK = 128
LN_EPS = 1e-5
DN_ALPHA = (2.0 * DEPTH) ** 0.25
DN_BETA = (8.0 * DEPTH) ** -0.25

kernel_name = "yoco_conformer_stickbreaking_hybrid"


def layer_norm(x, g, b):
    xf = x.astype(jnp.float32)
    mu = jnp.mean(xf, axis=-1, keepdims=True)
    var = jnp.mean(jnp.square(xf - mu), axis=-1, keepdims=True)
    y = (xf - mu) * lax.rsqrt(var + LN_EPS)
    return (y * g.astype(jnp.float32) + b.astype(jnp.float32)).astype(x.dtype)


def causal_dwconv(x, w, b):
    k = w.shape[0]
    y = lax.conv_general_dilated(
        x, w[:, None, :].astype(x.dtype), window_strides=(1,), padding=[(k - 1, 0)],
        dimension_numbers=("NWC", "WIO", "NWC"), feature_group_count=x.shape[-1])
    return y + b


def conformer_conv(x, pw1_w, pw1_b, dw_w, dw_b, ln_g, ln_b, pw2_w, pw2_b):
    h = x @ pw1_w + pw1_b
    a, g = jnp.split(h, 2, axis=-1)
    h = a * jax.nn.sigmoid(g)
    h = causal_dwconv(h, dw_w, dw_b)
    h = layer_norm(h, ln_g, ln_b)
    h = jax.nn.silu(h)
    return h @ pw2_w + pw2_b


def stick_breaking_attention(q, k, v):
    b, s, h, dh = q.shape
    nb = s // Q_BLOCK
    scale = 1.0 / math.sqrt(dh)
    kh = jnp.transpose(k, (0, 2, 1, 3))
    vh = jnp.transpose(v, (0, 2, 1, 3))
    qb = jnp.transpose(q.reshape(b, nb, Q_BLOCK, h, dh), (1, 0, 3, 2, 4))
    t0s = jnp.arange(nb, dtype=jnp.int32) * Q_BLOCK
    ts = jnp.arange(s, dtype=jnp.int32)

    def block(args):
        qblk, t0 = args
        z = jnp.einsum('bhqd,bhkd->bhqk', qblk, kh).astype(jnp.float32) * scale
        tq = t0 + jnp.arange(Q_BLOCK, dtype=jnp.int32)
        mask = ts[None, :] < tq[:, None]
        log_1m = jnp.where(mask, jax.nn.log_sigmoid(-z), 0.0)
        rev = lax.cumsum(log_1m, axis=3, reverse=True)
        log_a = jax.nn.log_sigmoid(z) + (rev - log_1m)
        a = jnp.where(mask, jnp.exp(log_a), 0.0)
        return jnp.einsum('bhqk,bhkd->bhqd', a.astype(vh.dtype), vh)

    out = lax.map(block, (qb, t0s))
    return jnp.transpose(out, (1, 0, 3, 2, 4)).reshape(b, s, h * dh)


def conv_gated_ffn(x, w_up, w_gate, conv_w, conv_b, w_down):
    u = x @ w_up
    g = causal_dwconv(x @ w_gate, conv_w, conv_b)
    return (jax.nn.silu(g) * u) @ w_down


def setup_inputs(seed: int = 0) -> dict:
    key = jax.random.key(seed)
    ks = jax.random.split(key, 32)
    D, F = D_MODEL, FFN_DIM

    def nrm(k, shape, scale):
        return jax.random.normal(k, shape, jnp.float32) * scale

    return {
        "x": nrm(ks[0], (BATCH, SEQ, D), 1.0),
        "p": nrm(ks[1], (DEPTH, BATCH, SEQ, PLE_DIM), 1.0),
        "a_pw1_w": nrm(ks[2], (N_A, D, 2 * D), D ** -0.5),
        "a_pw1_b": nrm(ks[3], (N_A, 2 * D), 0.02),
        "a_dw_w": nrm(ks[4], (N_A, CONV_W, D), CONV_W ** -0.5),
        "a_dw_b": nrm(ks[5], (N_A, D), 0.02),
        "a_ln_g": 1.0 + nrm(ks[6], (N_A, D), 0.02),
        "a_ln_b": nrm(ks[7], (N_A, D), 0.02),
        "a_pw2_w": nrm(ks[8], (N_A, D, D), D ** -0.5 * DN_BETA),
        "a_pw2_b": nrm(ks[9], (N_A, D), 0.02),
        "b_wq": nrm(ks[10], (N_B, D, D), D ** -0.5),
        "kv_wk": nrm(ks[11], (D, D), D ** -0.5),
        "kv_wv": nrm(ks[12], (D, D), D ** -0.5 * DN_BETA),
        "b_wo": nrm(ks[13], (N_B, D, D), D ** -0.5 * DN_BETA),
        "ln_mix_g": 1.0 + nrm(ks[14], (DEPTH, D), 0.02),
        "ln_mix_b": nrm(ks[15], (DEPTH, D), 0.02),
        "ffn_w_up": nrm(ks[16], (DEPTH, D, F), D ** -0.5),
        "ffn_w_gate": nrm(ks[17], (DEPTH, D, F), D ** -0.5),
        "ffn_conv_w": nrm(ks[18], (DEPTH, FFN_CONV_W, F), FFN_CONV_W ** -0.5),
        "ffn_conv_b": nrm(ks[19], (DEPTH, F), 0.02),
        "ffn_w_down": nrm(ks[20], (DEPTH, F, D), F ** -0.5 * DN_BETA),
        "ple_w_gate": nrm(ks[21], (DEPTH, D, D), D ** -0.5),
        "ple_w_proj": nrm(ks[22], (DEPTH, PLE_DIM, D), PLE_DIM ** -0.5 * DN_BETA),
        "ln_ffn_g": 1.0 + nrm(ks[23], (DEPTH, D), 0.02),
        "ln_ffn_b": nrm(ks[24], (DEPTH, D), 0.02),
    }


def reference(x, p, a_pw1_w, a_pw1_b, a_dw_w, a_dw_b, a_ln_g, a_ln_b, a_pw2_w, a_pw2_b,
              b_wq, kv_wk, kv_wv, b_wo, ln_mix_g, ln_mix_b,
              ffn_w_up, ffn_w_gate, ffn_conv_w, ffn_conv_b, ffn_w_down,
              ple_w_gate, ple_w_proj, ln_ffn_g, ln_ffn_b):
    b, s, d = x.shape
    k_shared = None
    v_shared = None
    for i in range(DEPTH):
        if i < N_A:
            mix = conformer_conv(x, a_pw1_w[i], a_pw1_b[i], a_dw_w[i], a_dw_b[i],
                                 a_ln_g[i], a_ln_b[i], a_pw2_w[i], a_pw2_b[i])
        else:
            j = i - N_A
            if k_shared is None:
                k_shared = (x @ kv_wk).reshape(b, s, N_HEADS, HEAD_DIM)
                v_shared = (x @ kv_wv).reshape(b, s, N_HEADS, HEAD_DIM)
            q = (x @ b_wq[j]).reshape(b, s, N_HEADS, HEAD_DIM)
            mix = stick_breaking_attention(q, k_shared, v_shared) @ b_wo[j]
        x = layer_norm(DN_ALPHA * x + mix, ln_mix_g[i], ln_mix_b[i])
        ffn = conv_gated_ffn(x, ffn_w_up[i], ffn_w_gate[i], ffn_conv_w[i], ffn_conv_b[i], ffn_w_down[i])
        ple = jax.nn.sigmoid(x @ ple_w_gate[i]) * (p[i] @ ple_w_proj[i])
        x = layer_norm(DN_ALPHA * x + ffn + ple, ln_ffn_g[i], ln_ffn_b[i])
    return x
```

```python
import functools

import jax
import jax.numpy as jnp
from jax import lax
from jax.experimental import pallas as pl
from jax.experimental.pallas import tpu as pltpu

F32 = jnp.float32
BF16 = jnp.bfloat16

LN_EPS = 1e-5
HEAD_DIM = 64
LANES = 128
SUBLANES = 8
ATT_BLOCK = 128
CONV_HALO = 32
F32_EXP_ZERO_BELOW = -104.0
VMEM_LIMIT_BYTES = 56 * 1024 * 1024


def _layer_norm(v, g, b):
    mu = jnp.mean(v, axis=-1, keepdims=True)
    c = v - mu
    var = jnp.mean(c * c, axis=-1, keepdims=True)
    return c * lax.rsqrt(var + LN_EPS) * g + b


def _dot(a, b):
    return jnp.dot(a, b, preferred_element_type=F32)


def _resident(shape):
    zeros = (0,) * len(shape)
    return pl.BlockSpec(shape, lambda b, s: zeros, pipeline_mode=pl.Buffered(1))


def _row_tile(tm, d):
    return pl.BlockSpec((1, tm, d), lambda b, s: (b, s, 0))


def _params():
    return pltpu.CompilerParams(dimension_semantics=("arbitrary", "arbitrary"),
                                vmem_limit_bytes=VMEM_LIMIT_BYTES)


def _conformer_body(x_ref, pw1_ref, pw1b_ref, dww_ref, dwb_ref, lng_ref, lnb_ref, pw2_ref,
                    pw2b_ref, mg_ref, mb_ref, o_ref, hbuf_ref, cbuf_ref, *, alpha, row_chunk,
                    lane_chunk):
    tm, d = x_ref.shape[1], x_ref.shape[2]
    conv_w = dww_ref.shape[0]

    @pl.when(pl.program_id(1) == 0)
    def _():
        hbuf_ref[0:CONV_HALO, :] = jnp.zeros((CONV_HALO, d), F32)

    x = x_ref[0]
    h2 = _dot(x.astype(BF16), pw1_ref[...]) + pw1b_ref[...]
    hbuf_ref[CONV_HALO:CONV_HALO + tm, :] = h2[:, :d] * jax.nn.sigmoid(h2[:, d:])

    offs = [CONV_HALO - (conv_w - 1) + k for k in range(conv_w)]

    def chunk(c, carry):
        r0 = pl.multiple_of(c * row_chunk, row_chunk)
        for l0 in range(0, d, lane_chunk):
            ls = slice(l0, l0 + lane_chunk)
            acc = None
            for b in range(SUBLANES):
                rows = row_chunk if b == 0 else row_chunk + SUBLANES
                part = None
                for k, off in enumerate(offs):
                    if off % SUBLANES != b:
                        continue
                    win = hbuf_ref[pl.ds(r0 + (off - b), rows), ls]
                    term = win * dww_ref[k:k + 1, ls]
                    part = term if part is None else part + term
                if part is None:
                    continue
                part = part[b:b + row_chunk, :]
                acc = part if acc is None else acc + part
            cbuf_ref[pl.ds(r0, row_chunk), ls] = acc + dwb_ref[:, ls]
        return carry

    lax.fori_loop(0, tm // row_chunk, chunk, 0)

    y = _layer_norm(cbuf_ref[...], lng_ref[...], lnb_ref[...])
    y = y * jax.nn.sigmoid(y)
    mix = _dot(y.astype(BF16), pw2_ref[...]) + pw2b_ref[...]
    o_ref[0] = _layer_norm(alpha * x + mix, mg_ref[...], mb_ref[...])
    hbuf_ref[0:CONV_HALO, :] = hbuf_ref[tm:tm + CONV_HALO, :]


def _conformer_layer(x, pw1_w, pw1_b, dw_w, dw_b, ln_g, ln_b, pw2_w, pw2_b, mix_g, mix_b, *,
                     alpha, tm):
    bsz, seq, d = x.shape
    conv_w = dw_w.shape[0]
    assert conv_w - 1 <= CONV_HALO <= tm and seq % tm == 0
    row = lambda v: v.reshape(1, -1)
    body = functools.partial(_conformer_body, alpha=alpha, row_chunk=64, lane_chunk=256)
    return pl.pallas_call(
        body,
        grid=(bsz, seq // tm),
        in_specs=[_row_tile(tm, d), _resident((d, 2 * d)), _resident((1, 2 * d)),
                  _resident((conv_w, d)), _resident((1, d)), _resident((1, d)), _resident((1, d)),
                  _resident((d, d)), _resident((1, d)), _resident((1, d)), _resident((1, d))],
        out_specs=_row_tile(tm, d),
        out_shape=jax.ShapeDtypeStruct((bsz, seq, d), F32),
        scratch_shapes=[pltpu.VMEM((CONV_HALO + tm, d), F32), pltpu.VMEM((tm, d), F32)],
        compiler_params=_params(),
        name="conformer_mixer",
    )(x, pw1_w.astype(BF16), row(pw1_b), dw_w, row(dw_b), row(ln_g), row(ln_b),
      pw2_w.astype(BF16), row(pw2_b), row(mix_g), row(mix_b))


def _ffn_body(x_ref, p_ref, wup_ref, wgate_ref, cw_ref, cb_ref, wdown_ref, pg_ref, pp_ref,
              g_ref, b_ref, o_ref, gprev_ref, gbuf_ref, acc_ref, *, alpha, f_chunk):
    tm = x_ref.shape[1]
    f = wup_ref.shape[1]
    conv_w = cw_ref.shape[0]

    @pl.when(pl.program_id(1) == 0)
    def _():
        gprev_ref[...] = jnp.zeros(gprev_ref.shape, F32)

    x = x_ref[0]
    xb = x.astype(BF16)
    for c in range(f // f_chunk):
        fs = slice(c * f_chunk, (c + 1) * f_chunk)
        u = _dot(xb, wup_ref[:, fs])
        g = _dot(xb, wgate_ref[:, fs])
        gbuf_ref[0:SUBLANES, :] = gprev_ref[:, fs]
        gbuf_ref[SUBLANES:SUBLANES + tm, :] = g
        gprev_ref[:, fs] = g[tm - SUBLANES:tm, :]
        gc = cb_ref[:, fs] + cw_ref[conv_w - 1:conv_w, fs] * g
        for k in range(conv_w - 1):
            off = SUBLANES - (conv_w - 1) + k
            gc = gc + cw_ref[k:k + 1, fs] * gbuf_ref[off:off + tm, :]
        hidden = (gc * jax.nn.sigmoid(gc) * u).astype(BF16)
        part = _dot(hidden, wdown_ref[fs, :])
        if c == 0:
            acc_ref[...] = part
        else:
            acc_ref[...] += part
    gate = jax.nn.sigmoid(_dot(xb, pg_ref[...]))
    proj = _dot(p_ref[0].astype(BF16), pp_ref[...])
    o_ref[0] = _layer_norm(alpha * x + acc_ref[...] + gate * proj, g_ref[...], b_ref[...])


def _ffn_layer(x, p, w_up, w_gate, conv_w, conv_b, w_down, ple_gate, ple_proj, ln_g, ln_b, *,
               alpha, tm, f_chunk):
    bsz, seq, d = x.shape
    f = w_up.shape[1]
    pd = p.shape[-1]
    kw = conv_w.shape[0]
    assert kw - 1 <= SUBLANES <= tm and seq % tm == 0 and f % f_chunk == 0
    row = lambda v: v.reshape(1, -1)
    body = functools.partial(_ffn_body, alpha=alpha, f_chunk=f_chunk)
    return pl.pallas_call(
        body,
        grid=(bsz, seq // tm),
        in_specs=[_row_tile(tm, d), _row_tile(tm, pd), _resident((d, f)), _resident((d, f)),
                  _resident((kw, f)), _resident((1, f)), _resident((f, d)), _resident((d, d)),
                  _resident((pd, d)), _resident((1, d)), _resident((1, d))],
        out_specs=_row_tile(tm, d),
        out_shape=jax.ShapeDtypeStruct((bsz, seq, d), F32),
        scratch_shapes=[pltpu.VMEM((SUBLANES, f), F32), pltpu.VMEM((SUBLANES + tm, f_chunk), F32),
                        pltpu.VMEM((tm, d), F32)],
        compiler_params=_params(),
        name="ffn_ple",
    )(x, p, w_up.astype(BF16), w_gate.astype(BF16), conv_w, row(conv_b), w_down.astype(BF16),
      ple_gate.astype(BF16), ple_proj.astype(BF16), row(ln_g), row(ln_b))


def _kv_body(x_ref, wkt_ref, wv_ref, kt_ref, v_ref):
    tm, d = x_ref.shape[1], x_ref.shape[2]
    xb = x_ref[0].astype(BF16)
    kt = lax.dot_general(wkt_ref[...], xb, (((1,), (1,)), ((), ())),
                         preferred_element_type=F32)
    v = _dot(xb, wv_ref[...])
    for j in range(tm // ATT_BLOCK):
        js = slice(j * ATT_BLOCK, (j + 1) * ATT_BLOCK)
        for p in range(d // LANES):
            ps = slice(p * LANES, (p + 1) * LANES)
            kt_ref[0, j, p] = kt[ps, js].astype(BF16)
            v_ref[0, j, p] = v[js, ps].astype(BF16)


def _kv_project(x, wk, wv, *, tm):
    bsz, seq, d = x.shape
    nkb, npairs = seq // ATT_BLOCK, d // LANES
    blocked = jax.ShapeDtypeStruct((bsz, nkb, npairs, ATT_BLOCK, LANES), BF16)
    spec = pl.BlockSpec((1, tm // ATT_BLOCK, npairs, ATT_BLOCK, LANES),
                        lambda b, s: (b, s, 0, 0, 0))
    return pl.pallas_call(
        _kv_body,
        grid=(bsz, seq // tm),
        in_specs=[_row_tile(tm, d), _resident((d, d)), _resident((d, d))],
        out_specs=[spec, spec],
        out_shape=[blocked, blocked],
        compiler_params=_params(),
        name="kv_project",
    )(x, wk.T.astype(BF16), wv.astype(BF16))


def _attn_body(x_ref, wq_ref, kt_ref, v_ref, wo_ref, mg_ref, mb_ref, o_ref,
               q_scr, out_scr, tri_scr, carry_scr, acc_scr, *, alpha):
    tm, d = x_ref.shape[1], x_ref.shape[2]
    npairs = d // LANES
    nqb = tm // ATT_BLOCK
    blk = ATT_BLOCK
    seq_tile = pl.program_id(1)

    x = x_ref[0]
    q = _dot(x.astype(BF16), wq_ref[...])
    for p in range(npairs):
        q_scr[p] = q[:, p * LANES:(p + 1) * LANES].astype(BF16)

    rr = lax.broadcasted_iota(jnp.int32, (2 * blk, 2 * blk), 0) % blk
    cc = lax.broadcasted_iota(jnp.int32, (2 * blk, 2 * blk), 1)
    tri_scr[...] = jnp.where((cc >= blk) | (rr > cc), 1.0, 0.0).astype(BF16)

    lane = lax.broadcasted_iota(jnp.int32, (blk, LANES), 1)
    even_head = lane < HEAD_DIM
    row2 = lax.broadcasted_iota(jnp.int32, (2 * blk, blk), 0) % blk
    col2 = lax.broadcasted_iota(jnp.int32, (2 * blk, blk), 1)
    causal = col2 < row2

    def sweep_block(j, p, qcat, diagonal):
        kt = kt_ref[0, j, p]
        vv = v_ref[0, j, p]
        z = _dot(qcat, kt)
        sp = jnp.maximum(z, 0.0) + jnp.log(1.0 + jnp.exp(-jnp.abs(z)))
        log_1m = -sp
        if diagonal:
            log_1m = jnp.where(causal, log_1m, 0.0)
        hi = log_1m.astype(BF16)
        lo = (log_1m - hi.astype(F32)).astype(BF16)
        sums = _dot(jnp.concatenate([hi, lo], axis=1), tri_scr[...])
        log_a = (z - sp) + sums[:, :blk] + carry_scr[...]
        a = jnp.exp(log_a)
        if diagonal:
            a = jnp.where(causal, a, 0.0)
        a = a.astype(BF16)
        acat = jnp.concatenate([a[:blk], a[blk:]], axis=1)
        zero = jnp.zeros_like(vv)
        vcat = jnp.concatenate([jnp.where(even_head, vv, zero),
                                jnp.where(even_head, zero, vv)], axis=0)
        new_carry = carry_scr[...] + sums[:, blk:]
        carry_scr[...] = new_carry
        acc_scr[...] += _dot(acat, vcat)
        return jnp.max(new_carry)

    def per_block_pair(idx, c):
        qi = idx // npairs
        p = idx % npairs
        r0 = pl.multiple_of(qi * blk, blk)
        qp = q_scr[p, pl.ds(r0, blk), :]
        zero = jnp.zeros_like(qp)
        qcat = jnp.concatenate([jnp.where(even_head, qp, zero),
                                jnp.where(even_head, zero, qp)], axis=0)
        carry_scr[...] = jnp.zeros(carry_scr.shape, F32)
        acc_scr[...] = jnp.zeros(acc_scr.shape, F32)
        jq = seq_tile * nqb + qi
        top = sweep_block(jq, p, qcat, True)

        def cond(state):
            j, top = state
            return (j >= 0) & (top >= F32_EXP_ZERO_BELOW)

        def step(state):
            j, _ = state
            return j - 1, sweep_block(j, p, qcat, False)

        lax.while_loop(cond, step, (jq - 1, top))
        out_scr[p, pl.ds(r0, blk), :] = acc_scr[...].astype(BF16)
        return c

    lax.fori_loop(0, nqb * npairs, per_block_pair, 0)

    attn = jnp.concatenate([out_scr[p] for p in range(npairs)], axis=1)
    mix = _dot(attn, wo_ref[...])
    o_ref[0] = _layer_norm(alpha * x + mix, mg_ref[...], mb_ref[...])


def _attention_layer(x, wq, kt, v, wo, mix_g, mix_b, *, alpha, tm):
    bsz, seq, d = x.shape
    nkb, npairs = seq // ATT_BLOCK, d // LANES
    assert seq % tm == 0 and tm % ATT_BLOCK == 0 and d % LANES == 0
    row = lambda a: a.reshape(1, -1)
    kv_spec = pl.BlockSpec((1, nkb, npairs, ATT_BLOCK, LANES), lambda b, s: (b, 0, 0, 0, 0),
                           pipeline_mode=pl.Buffered(1))
    scale = HEAD_DIM ** -0.5
    body = functools.partial(_attn_body, alpha=alpha)
    return pl.pallas_call(
        body,
        grid=(bsz, seq // tm),
        in_specs=[_row_tile(tm, d), _resident((d, d)), kv_spec, kv_spec, _resident((d, d)),
                  _resident((1, d)), _resident((1, d))],
        out_specs=_row_tile(tm, d),
        out_shape=jax.ShapeDtypeStruct((bsz, seq, d), F32),
        scratch_shapes=[pltpu.VMEM((npairs, tm, LANES), BF16),
                        pltpu.VMEM((npairs, tm, LANES), BF16),
                        pltpu.VMEM((2 * ATT_BLOCK, 2 * ATT_BLOCK), BF16),
                        pltpu.VMEM((2 * ATT_BLOCK, ATT_BLOCK), F32),
                        pltpu.VMEM((ATT_BLOCK, LANES), F32)],
        compiler_params=_params(),
        name="stickbreak_attention",
    )(x, (wq * scale).astype(BF16), kt, v, wo.astype(BF16), row(mix_g), row(mix_b))


def kernel(x, p, a_pw1_w, a_pw1_b, a_dw_w, a_dw_b, a_ln_g, a_ln_b, a_pw2_w, a_pw2_b, b_wq, kv_wk, kv_wv, b_wo, ln_mix_g, ln_mix_b, ffn_w_up, ffn_w_gate, ffn_conv_w, ffn_conv_b, ffn_w_down, ple_w_gate, ple_w_proj, ln_ffn_g, ln_ffn_b):
    depth = p.shape[0]
    n_conf = a_pw1_w.shape[0]
    seq = x.shape[1]
    alpha = (2.0 * depth) ** 0.25
    tm = min(512, seq)
    f = ffn_w_up.shape[-1]
    f_chunk = 256 if f % 256 == 0 else LANES
    kt = v = None
    for i in range(depth):
        if i < n_conf:
            x = _conformer_layer(x, a_pw1_w[i], a_pw1_b[i], a_dw_w[i], a_dw_b[i], a_ln_g[i],
                                 a_ln_b[i], a_pw2_w[i], a_pw2_b[i], ln_mix_g[i], ln_mix_b[i],
                                 alpha=alpha, tm=tm)
        else:
            j = i - n_conf
            if kt is None:
                kt, v = _kv_project(x, kv_wk, kv_wv, tm=tm)
            x = _attention_layer(x, b_wq[j], kt, v, b_wo[j], ln_mix_g[i], ln_mix_b[i],
                                 alpha=alpha, tm=tm)
        x = _ffn_layer(x, p[i], ffn_w_up[i], ffn_w_gate[i], ffn_conv_w[i], ffn_conv_b[i],
                       ffn_w_down[i], ple_w_gate[i], ple_w_proj[i], ln_ffn_g[i], ln_ffn_b[i],
                       alpha=alpha, tm=tm, f_chunk=f_chunk)
    return x
```

```python
import functools
import math

import jax
import jax.numpy as jnp
from jax import lax
from jax.experimental import pallas as pl
from jax.experimental.pallas import tpu as pltpu

F32 = jnp.float32
BF16 = jnp.bfloat16

LN_EPS = 1e-5
HEAD_DIM = 64
LANES = 128
SUBLANES = 8
ATT_BLOCK = 128
CONV_HALO = 32
LOG2_E = 1.4426950408889634
F32_EXP2_ZERO_BELOW = -151.0
ATT_PAIR_GROUP = 8
VMEM_LIMIT_BYTES = 56 * 1024 * 1024


def _layer_norm(v, g, b):
    mu = jnp.mean(v, axis=-1, keepdims=True)
    c = v - mu
    var = jnp.mean(c * c, axis=-1, keepdims=True)
    return c * lax.rsqrt(var + LN_EPS) * g + b


def _dot(a, b):
    return jnp.dot(a, b, preferred_element_type=F32)


def _resident(shape):
    zeros = (0,) * len(shape)
    return pl.BlockSpec(shape, lambda b, s: zeros, pipeline_mode=pl.Buffered(1))


def _row_tile(tm, d):
    return pl.BlockSpec((1, tm, d), lambda b, s: (b, s, 0))


def _params():
    return pltpu.CompilerParams(dimension_semantics=("arbitrary", "arbitrary"),
                                vmem_limit_bytes=VMEM_LIMIT_BYTES)


def _conformer_body(x_ref, pw1_ref, pw1b_ref, dww_ref, dwb_ref, lng_ref, lnb_ref, pw2_ref,
                    pw2b_ref, mg_ref, mb_ref, o_ref, hbuf_ref, cbuf_ref, *, alpha, row_chunk,
                    lane_chunk):
    tm, d = x_ref.shape[1], x_ref.shape[2]
    conv_w = dww_ref.shape[0]

    @pl.when(pl.program_id(1) == 0)
    def _():
        hbuf_ref[0:CONV_HALO, :] = jnp.zeros((CONV_HALO, d), F32)

    x = x_ref[0]
    h2 = _dot(x.astype(BF16), pw1_ref[...]) + pw1b_ref[...]
    hbuf_ref[CONV_HALO:CONV_HALO + tm, :] = h2[:, :d] * jax.nn.sigmoid(h2[:, d:])

    offs = [CONV_HALO - (conv_w - 1) + k for k in range(conv_w)]

    def chunk(c, carry):
        r0 = pl.multiple_of(c * row_chunk, row_chunk)
        for l0 in range(0, d, lane_chunk):
            ls = slice(l0, l0 + lane_chunk)
            acc = None
            for b in range(SUBLANES):
                rows = row_chunk if b == 0 else row_chunk + SUBLANES
                part = None
                for k, off in enumerate(offs):
                    if off % SUBLANES != b:
                        continue
                    win = hbuf_ref[pl.ds(r0 + (off - b), rows), ls]
                    term = win * dww_ref[k:k + 1, ls]
                    part = term if part is None else part + term
                if part is None:
                    continue
                part = part[b:b + row_chunk, :]
                acc = part if acc is None else acc + part
            cbuf_ref[pl.ds(r0, row_chunk), ls] = acc + dwb_ref[:, ls]
        return carry

    lax.fori_loop(0, tm // row_chunk, chunk, 0)

    y = _layer_norm(cbuf_ref[...], lng_ref[...], lnb_ref[...])
    y = y * jax.nn.sigmoid(y)
    mix = _dot(y.astype(BF16), pw2_ref[...]) + pw2b_ref[...]
    o_ref[0] = _layer_norm(alpha * x + mix, mg_ref[...], mb_ref[...])
    hbuf_ref[0:CONV_HALO, :] = hbuf_ref[tm:tm + CONV_HALO, :]


def _conformer_layer(x, pw1_w, pw1_b, dw_w, dw_b, ln_g, ln_b, pw2_w, pw2_b, mix_g, mix_b, *,
                     alpha, tm):
    bsz, seq, d = x.shape
    conv_w = dw_w.shape[0]
    assert conv_w - 1 <= CONV_HALO <= tm and seq % tm == 0
    row = lambda v: v.reshape(1, -1)
    body = functools.partial(_conformer_body, alpha=alpha, row_chunk=64, lane_chunk=256)
    return pl.pallas_call(
        body,
        grid=(bsz, seq // tm),
        in_specs=[_row_tile(tm, d), _resident((d, 2 * d)), _resident((1, 2 * d)),
                  _resident((conv_w, d)), _resident((1, d)), _resident((1, d)), _resident((1, d)),
                  _resident((d, d)), _resident((1, d)), _resident((1, d)), _resident((1, d))],
        out_specs=_row_tile(tm, d),
        out_shape=jax.ShapeDtypeStruct((bsz, seq, d), F32),
        scratch_shapes=[pltpu.VMEM((CONV_HALO + tm, d), F32), pltpu.VMEM((tm, d), F32)],
        compiler_params=_params(),
        name="conformer_mixer",
    )(x, pw1_w.astype(BF16), row(pw1_b), dw_w, row(dw_b), row(ln_g), row(ln_b),
      pw2_w.astype(BF16), row(pw2_b), row(mix_g), row(mix_b))


def _ffn_body(x_ref, p_ref, wup_ref, wgate_ref, cw_ref, cb_ref, wdown_ref, pg_ref, pp_ref,
              g_ref, b_ref, o_ref, gprev_ref, gbuf_ref, acc_ref, *, alpha, f_chunk):
    tm = x_ref.shape[1]
    f = wup_ref.shape[1]
    conv_w = cw_ref.shape[0]

    @pl.when(pl.program_id(1) == 0)
    def _():
        gprev_ref[...] = jnp.zeros(gprev_ref.shape, F32)

    x = x_ref[0]
    xb = x.astype(BF16)
    for c in range(f // f_chunk):
        fs = slice(c * f_chunk, (c + 1) * f_chunk)
        u = _dot(xb, wup_ref[:, fs])
        g = _dot(xb, wgate_ref[:, fs])
        gbuf_ref[0:SUBLANES, :] = gprev_ref[:, fs]
        gbuf_ref[SUBLANES:SUBLANES + tm, :] = g
        gprev_ref[:, fs] = g[tm - SUBLANES:tm, :]
        gc = cb_ref[:, fs] + cw_ref[conv_w - 1:conv_w, fs] * g
        for k in range(conv_w - 1):
            off = SUBLANES - (conv_w - 1) + k
            gc = gc + cw_ref[k:k + 1, fs] * gbuf_ref[off:off + tm, :]
        hidden = (gc * jax.nn.sigmoid(gc) * u).astype(BF16)
        part = _dot(hidden, wdown_ref[fs, :])
        if c == 0:
            acc_ref[...] = part
        else:
            acc_ref[...] += part
    gate = jax.nn.sigmoid(_dot(xb, pg_ref[...]))
    proj = _dot(p_ref[0].astype(BF16), pp_ref[...])
    o_ref[0] = _layer_norm(alpha * x + acc_ref[...] + gate * proj, g_ref[...], b_ref[...])


def _ffn_layer(x, p, w_up, w_gate, conv_w, conv_b, w_down, ple_gate, ple_proj, ln_g, ln_b, *,
               alpha, tm, f_chunk):
    bsz, seq, d = x.shape
    f = w_up.shape[1]
    pd = p.shape[-1]
    kw = conv_w.shape[0]
    assert kw - 1 <= SUBLANES <= tm and seq % tm == 0 and f % f_chunk == 0
    row = lambda v: v.reshape(1, -1)
    body = functools.partial(_ffn_body, alpha=alpha, f_chunk=f_chunk)
    return pl.pallas_call(
        body,
        grid=(bsz, seq // tm),
        in_specs=[_row_tile(tm, d), _row_tile(tm, pd), _resident((d, f)), _resident((d, f)),
                  _resident((kw, f)), _resident((1, f)), _resident((f, d)), _resident((d, d)),
                  _resident((pd, d)), _resident((1, d)), _resident((1, d))],
        out_specs=_row_tile(tm, d),
        out_shape=jax.ShapeDtypeStruct((bsz, seq, d), F32),
        scratch_shapes=[pltpu.VMEM((SUBLANES, f), F32), pltpu.VMEM((SUBLANES + tm, f_chunk), F32),
                        pltpu.VMEM((tm, d), F32)],
        compiler_params=_params(),
        name="ffn_ple",
    )(x, p, w_up.astype(BF16), w_gate.astype(BF16), conv_w, row(conv_b), w_down.astype(BF16),
      ple_gate.astype(BF16), ple_proj.astype(BF16), row(ln_g), row(ln_b))


def _kv_body(x_ref, wkt_ref, wv_ref, kt_ref, v_ref):
    tm, d = x_ref.shape[1], x_ref.shape[2]
    xb = x_ref[0].astype(BF16)
    kt = lax.dot_general(wkt_ref[...], xb, (((1,), (1,)), ((), ())),
                         preferred_element_type=F32)
    v = _dot(xb, wv_ref[...])
    for j in range(tm // ATT_BLOCK):
        js = slice(j * ATT_BLOCK, (j + 1) * ATT_BLOCK)
        for p in range(d // LANES):
            ps = slice(p * LANES, (p + 1) * LANES)
            kt_ref[0, j, p] = kt[ps, js].astype(BF16)
            v_ref[0, j, p] = v[js, ps].astype(BF16)


def _kv_project(x, wk, wv, *, tm):
    bsz, seq, d = x.shape
    nkb, npairs = seq // ATT_BLOCK, d // LANES
    blocked = jax.ShapeDtypeStruct((bsz, nkb, npairs, ATT_BLOCK, LANES), BF16)
    spec = pl.BlockSpec((1, tm // ATT_BLOCK, npairs, ATT_BLOCK, LANES),
                        lambda b, s: (b, s, 0, 0, 0))
    return pl.pallas_call(
        _kv_body,
        grid=(bsz, seq // tm),
        in_specs=[_row_tile(tm, d), _resident((d, d)), _resident((d, d))],
        out_specs=[spec, spec],
        out_shape=[blocked, blocked],
        compiler_params=_params(),
        name="kv_project",
    )(x, wk.T.astype(BF16), wv.astype(BF16))


def _attn_body(x_ref, wq_ref, kt_ref, v_ref, wo_ref, mg_ref, mb_ref, o_ref,
               q_scr, out_scr, tri_scr, carry_scr, acc_scr, *, alpha, pair_group):
    tm, d = x_ref.shape[1], x_ref.shape[2]
    npairs = d // LANES
    nqb = tm // ATT_BLOCK
    blk = ATT_BLOCK
    seq_tile = pl.program_id(1)

    x = x_ref[0]
    q = _dot(x.astype(BF16), wq_ref[...])
    for p in range(npairs):
        q_scr[p] = q[:, p * LANES:(p + 1) * LANES].astype(BF16)

    rr = lax.broadcasted_iota(jnp.int32, (2 * blk, 2 * blk), 0) % blk
    cc = lax.broadcasted_iota(jnp.int32, (2 * blk, 2 * blk), 1)
    tri_scr[...] = jnp.where((cc >= blk) | (rr > cc), 1.0, 0.0).astype(BF16)

    lane = lax.broadcasted_iota(jnp.int32, (blk, LANES), 1)
    even_head = lane < HEAD_DIM
    row2 = lax.broadcasted_iota(jnp.int32, (2 * blk, blk), 0) % blk
    col2 = lax.broadcasted_iota(jnp.int32, (2 * blk, blk), 1)
    causal = col2 < row2

    def split_heads(t, axis):
        zero = jnp.zeros_like(t)
        return jnp.concatenate([jnp.where(even_head, t, zero), jnp.where(even_head, zero, t)],
                               axis=axis)

    def sweep_block(j, pairs, qcats, diagonal):
        n = len(pairs)
        zs = [_dot(qcats[g], kt_ref[0, j, pairs[g]]) for g in range(n)]
        log_bs, lcats = [], []
        for z in zs:
            m = jnp.minimum(z, 0.0)
            soft = jnp.log(1.0 + jnp.exp2(m + m - z)) * LOG2_E
            log_b = m - soft
            log_1m = log_b - z
            if diagonal:
                log_1m = jnp.where(causal, log_1m, 0.0)
            hi = log_1m.astype(BF16)
            lo = (log_1m - hi.astype(F32)).astype(BF16)
            log_bs.append(log_b)
            lcats.append(jnp.concatenate([hi, lo], axis=1))
        sums = [_dot(lcat, tri_scr[...]) for lcat in lcats]
        top = None
        acats = []
        for g in range(n):
            carry = carry_scr[g]
            a = jnp.exp2(log_bs[g] + sums[g][:, :blk] + carry)
            if diagonal:
                a = jnp.where(causal, a, 0.0)
            a = a.astype(BF16)
            acats.append(jnp.concatenate([a[:blk], a[blk:]], axis=1))
            carry = carry + sums[g][:, blk:]
            carry_scr[g] = carry
            top = carry if top is None else jnp.maximum(top, carry)
        for g in range(n):
            acc_scr[g] += _dot(acats[g], split_heads(v_ref[0, j, pairs[g]], 0))
        return jnp.max(top)

    for p0 in range(0, npairs, pair_group):
        pairs = tuple(range(p0, p0 + pair_group))

        def per_query_block(qi, c, pairs=pairs):
            r0 = pl.multiple_of(qi * blk, blk)
            qcats = [split_heads(q_scr[p, pl.ds(r0, blk), :], 0) for p in pairs]
            carry_scr[...] = jnp.zeros(carry_scr.shape, F32)
            acc_scr[...] = jnp.zeros(acc_scr.shape, F32)
            jq = seq_tile * nqb + qi
            top = sweep_block(jq, pairs, qcats, True)

            def cond(state):
                j, top = state
                return (j >= 0) & (top >= F32_EXP2_ZERO_BELOW)

            def step(state, pairs=pairs, qcats=qcats):
                j, _ = state
                return j - 1, sweep_block(j, pairs, qcats, False)

            lax.while_loop(cond, step, (jq - 1, top))
            for g, p in enumerate(pairs):
                out_scr[p, pl.ds(r0, blk), :] = acc_scr[g].astype(BF16)
            return c

        lax.fori_loop(0, nqb, per_query_block, 0)

    attn = jnp.concatenate([out_scr[p] for p in range(npairs)], axis=1)
    mix = _dot(attn, wo_ref[...])
    o_ref[0] = _layer_norm(alpha * x + mix, mg_ref[...], mb_ref[...])


def _attention_layer(x, wq, kt, v, wo, mix_g, mix_b, *, alpha, tm):
    bsz, seq, d = x.shape
    nkb, npairs = seq // ATT_BLOCK, d // LANES
    assert seq % tm == 0 and tm % ATT_BLOCK == 0 and d % LANES == 0
    row = lambda a: a.reshape(1, -1)
    kv_spec = pl.BlockSpec((1, nkb, npairs, ATT_BLOCK, LANES), lambda b, s: (b, 0, 0, 0, 0),
                           pipeline_mode=pl.Buffered(1))
    pair_group = math.gcd(npairs, ATT_PAIR_GROUP)
    scale = HEAD_DIM ** -0.5 * LOG2_E
    body = functools.partial(_attn_body, alpha=alpha, pair_group=pair_group)
    return pl.pallas_call(
        body,
        grid=(bsz, seq // tm),
        in_specs=[_row_tile(tm, d), _resident((d, d)), kv_spec, kv_spec, _resident((d, d)),
                  _resident((1, d)), _resident((1, d))],
        out_specs=_row_tile(tm, d),
        out_shape=jax.ShapeDtypeStruct((bsz, seq, d), F32),
        scratch_shapes=[pltpu.VMEM((npairs, tm, LANES), BF16),
                        pltpu.VMEM((npairs, tm, LANES), BF16),
                        pltpu.VMEM((2 * ATT_BLOCK, 2 * ATT_BLOCK), BF16),
                        pltpu.VMEM((pair_group, 2 * ATT_BLOCK, ATT_BLOCK), F32),
                        pltpu.VMEM((pair_group, ATT_BLOCK, LANES), F32)],
        compiler_params=_params(),
        name="stickbreak_attention",
    )(x, (wq * scale).astype(BF16), kt, v, wo.astype(BF16), row(mix_g), row(mix_b))


def kernel(x, p, a_pw1_w, a_pw1_b, a_dw_w, a_dw_b, a_ln_g, a_ln_b, a_pw2_w, a_pw2_b, b_wq, kv_wk, kv_wv, b_wo, ln_mix_g, ln_mix_b, ffn_w_up, ffn_w_gate, ffn_conv_w, ffn_conv_b, ffn_w_down, ple_w_gate, ple_w_proj, ln_ffn_g, ln_ffn_b):
    depth = p.shape[0]
    n_conf = a_pw1_w.shape[0]
    seq = x.shape[1]
    alpha = (2.0 * depth) ** 0.25
    tm = min(512, seq)
    f = ffn_w_up.shape[-1]
    f_chunk = 256 if f % 256 == 0 else LANES
    kt = v = None
    for i in range(depth):
        if i < n_conf:
            x = _conformer_layer(x, a_pw1_w[i], a_pw1_b[i], a_dw_w[i], a_dw_b[i], a_ln_g[i],
                                 a_ln_b[i], a_pw2_w[i], a_pw2_b[i], ln_mix_g[i], ln_mix_b[i],
                                 alpha=alpha, tm=tm)
        else:
            j = i - n_conf
            if kt is None:
                kt, v = _kv_project(x, kv_wk, kv_wv, tm=tm)
            x = _attention_layer(x, b_wq[j], kt, v, b_wo[j], ln_mix_g[i], ln_mix_b[i],
                                 alpha=alpha, tm=tm)
        x = _ffn_layer(x, p[i], ffn_w_up[i], ffn_w_gate[i], ffn_conv_w[i], ffn_conv_b[i],
                       ffn_w_down[i], ple_w_gate[i], ple_w_proj[i], ln_ffn_g[i], ln_ffn_b[i],
                       alpha=alpha, tm=tm, f_chunk=f_chunk)
    return x
```

```python
import functools
import math

import jax
import jax.numpy as jnp
from jax import lax
from jax.experimental import pallas as pl
from jax.experimental.pallas import tpu as pltpu

F32 = jnp.float32
BF16 = jnp.bfloat16

LN_EPS = 1e-5
HEAD_DIM = 64
LANES = 128
SUBLANES = 8
ATT_BLOCK = 128
CONV_HALO = 32
LOG2_E = 1.4426950408889634
F32_EXP2_ZERO_BELOW = -151.0
FFN_LOOKAHEAD = 2
ATT_PAIR_GROUP = 8
VMEM_LIMIT_BYTES = 56 * 1024 * 1024


def _layer_norm(v, g, b):
    mu = jnp.mean(v, axis=-1, keepdims=True)
    c = v - mu
    var = jnp.mean(c * c, axis=-1, keepdims=True)
    return c * lax.rsqrt(var + LN_EPS) * g + b


def _dot(a, b):
    return jnp.dot(a, b, preferred_element_type=F32)


def _resident(shape):
    zeros = (0,) * len(shape)
    return pl.BlockSpec(shape, lambda b, s: zeros, pipeline_mode=pl.Buffered(1))


def _row_tile(tm, d):
    return pl.BlockSpec((1, tm, d), lambda b, s: (b, s, 0))


def _params():
    return pltpu.CompilerParams(dimension_semantics=("arbitrary", "arbitrary"),
                                vmem_limit_bytes=VMEM_LIMIT_BYTES)


def _conformer_body(x_ref, pw1_ref, pw1b_ref, dww_ref, dwb_ref, lng_ref, lnb_ref, pw2_ref,
                    pw2b_ref, mg_ref, mb_ref, o_ref, hbuf_ref, cbuf_ref, *, alpha):
    tm, d = x_ref.shape[1], x_ref.shape[2]
    nslab, conv_w = dww_ref.shape[0], dww_ref.shape[1]

    @pl.when(pl.program_id(1) == 0)
    def _():
        hbuf_ref[:, 0:CONV_HALO, :] = jnp.zeros((nslab, CONV_HALO, LANES), F32)

    x = x_ref[0]
    h2 = _dot(x.astype(BF16), pw1_ref[...]) + pw1b_ref[...]
    h = h2[:, :d] * jax.nn.sigmoid(h2[:, d:])
    for s in range(nslab):
        hbuf_ref[s, CONV_HALO:CONV_HALO + tm, :] = h[:, s * LANES:(s + 1) * LANES]

    def conv_slab(s, carry):
        taps = [jnp.broadcast_to(dww_ref[s, k:k + 1, :], (SUBLANES, LANES)) for k in range(conv_w)]
        bias = jnp.broadcast_to(dwb_ref[s], (SUBLANES, LANES))
        for base, stride in _strided_row_passes(tm):
            for v in range(base, base + stride):
                acc = [bias, None]
                for k in range(conv_w):
                    start = v + (CONV_HALO - (conv_w - 1) + k)
                    term = hbuf_ref[s, pl.ds(start, SUBLANES, stride=stride), :] * taps[k]
                    acc[k % 2] = term if acc[k % 2] is None else acc[k % 2] + term
                cbuf_ref[s, pl.ds(v, SUBLANES, stride=stride), :] = acc[0] + acc[1]
        return carry

    lax.fori_loop(0, nslab, conv_slab, 0)

    conv = jnp.concatenate([cbuf_ref[s] for s in range(nslab)], axis=1)
    y = _layer_norm(conv, lng_ref[...], lnb_ref[...])
    y = y * jax.nn.sigmoid(y)
    mix = _dot(y.astype(BF16), pw2_ref[...]) + pw2b_ref[...]
    o_ref[0] = _layer_norm(alpha * x + mix, mg_ref[...], mb_ref[...])
    hbuf_ref[:, 0:CONV_HALO, :] = hbuf_ref[:, tm:tm + CONV_HALO, :]


def _strided_row_passes(n_rows):
    n = n_rows // SUBLANES
    if n % 8 != 0:
        return [(0, n)]
    return [(0, n - 4), (SUBLANES * (n - 4), 4)]


def _conformer_layer(x, pw1_w, pw1_b, dw_w, dw_b, ln_g, ln_b, pw2_w, pw2_b, mix_g, mix_b, *,
                     alpha, tm):
    bsz, seq, d = x.shape
    conv_w = dw_w.shape[0]
    assert conv_w - 1 <= CONV_HALO <= tm and seq % tm == 0 and d % LANES == 0
    nslab = d // LANES
    row = lambda v: v.reshape(1, -1)
    slabs = lambda v: v.reshape(-1, nslab, LANES).transpose(1, 0, 2)
    body = functools.partial(_conformer_body, alpha=alpha)
    return pl.pallas_call(
        body,
        grid=(bsz, seq // tm),
        in_specs=[_row_tile(tm, d), _resident((d, 2 * d)), _resident((1, 2 * d)),
                  _resident((nslab, conv_w, LANES)), _resident((nslab, 1, LANES)),
                  _resident((1, d)), _resident((1, d)),
                  _resident((d, d)), _resident((1, d)), _resident((1, d)), _resident((1, d))],
        out_specs=_row_tile(tm, d),
        out_shape=jax.ShapeDtypeStruct((bsz, seq, d), F32),
        scratch_shapes=[pltpu.VMEM((nslab, CONV_HALO + tm, LANES), F32),
                        pltpu.VMEM((nslab, tm, LANES), F32)],
        compiler_params=_params(),
        name="conformer_mixer",
    )(x, pw1_w.astype(BF16), row(pw1_b), slabs(dw_w), slabs(dw_b), row(ln_g), row(ln_b),
      pw2_w.astype(BF16), row(pw2_b), row(mix_g), row(mix_b))


def _ffn_body(x_ref, p_ref, wup_ref, wgate_ref, cw_ref, cb_ref, wdown_ref, pg_ref, pp_ref,
              g_ref, b_ref, o_ref, gprev_ref, gbuf_ref, acc_ref, ple_ref, *, alpha, f_chunk):
    n_sub, sub = acc_ref.shape[0], acc_ref.shape[1]
    f = wup_ref.shape[1]
    conv_w = cw_ref.shape[0]
    nchunks = f // f_chunk

    @pl.when(pl.program_id(1) == 0)
    def _():
        gprev_ref[...] = jnp.zeros(gprev_ref.shape, F32)

    n_slices = min(nchunks - 1, 8) if sub % 64 == 0 else 1
    slice_rows = sub // n_slices

    def finish_rows(t, i):
        rs = slice(i * slice_rows, (i + 1) * slice_rows)
        out_rows = slice(t * sub + i * slice_rows, t * sub + (i + 1) * slice_rows)
        gate = jax.nn.sigmoid(ple_ref[0, rs, :])
        y = alpha * x_ref[0, out_rows, :] + acc_ref[t, rs, :] + gate * ple_ref[1, rs, :]
        o_ref[0, out_rows, :] = _layer_norm(y, g_ref[...], b_ref[...])

    for t in range(n_sub):
        rows = slice(t * sub, (t + 1) * sub)
        xb = x_ref[0, rows, :].astype(BF16)

        def up_gate(c, xb=xb):
            fs = slice(c * f_chunk, (c + 1) * f_chunk)
            return _dot(xb, wup_ref[:, fs]), _dot(xb, wgate_ref[:, fs])

        queue = [up_gate(c) for c in range(min(FFN_LOOKAHEAD, nchunks))]
        for c in range(nchunks):
            fs = slice(c * f_chunk, (c + 1) * f_chunk)
            if c + FFN_LOOKAHEAD < nchunks:
                queue.append(up_gate(c + FFN_LOOKAHEAD))
            u, g = queue.pop(0)
            if t > 0 and c < n_slices:
                finish_rows(t - 1, c)
            if c + 1 == nchunks:
                ple_ref[0] = _dot(xb, pg_ref[...])
                ple_ref[1] = _dot(p_ref[0, 0, rows, :].astype(BF16), pp_ref[...])
            gbuf_ref[0:SUBLANES, :] = gprev_ref[:, fs]
            gbuf_ref[SUBLANES:SUBLANES + sub, :] = g
            gprev_ref[:, fs] = g[sub - SUBLANES:sub, :]
            gc = cb_ref[:, fs] + cw_ref[conv_w - 1:conv_w, fs] * g
            for k in range(conv_w - 1):
                off = SUBLANES - (conv_w - 1) + k
                gc = gc + cw_ref[k:k + 1, fs] * gbuf_ref[off:off + sub, :]
            hidden = (gc * jax.nn.sigmoid(gc) * u).astype(BF16)
            part = _dot(hidden, wdown_ref[fs, :])
            if c == 0:
                acc_ref[t] = part
            else:
                acc_ref[t] += part
    for i in range(n_slices):
        finish_rows(n_sub - 1, i)


def _ffn_layer(x, p, layer, w_up, w_gate, conv_w, conv_b, w_down, ple_gate, ple_proj, ln_g,
               ln_b, *, alpha, tm, n_sub, f_chunk):
    bsz, seq, d = x.shape
    f = w_up.shape[1]
    pd = p.shape[-1]
    kw = conv_w.shape[0]
    sub = tm // n_sub
    assert kw - 1 <= SUBLANES <= sub and seq % tm == 0 and tm % n_sub == 0 and f % f_chunk == 0
    row = lambda v: v.reshape(1, -1)
    body = functools.partial(_ffn_body, alpha=alpha, f_chunk=f_chunk)
    p_spec = pl.BlockSpec((1, 1, tm, pd), lambda b, s: (layer, b, s, 0))
    return pl.pallas_call(
        body,
        grid=(bsz, seq // tm),
        in_specs=[_row_tile(tm, d), p_spec, _resident((d, f)), _resident((d, f)),
                  _resident((kw, f)), _resident((1, f)), _resident((f, d)), _resident((d, d)),
                  _resident((pd, d)), _resident((1, d)), _resident((1, d))],
        out_specs=_row_tile(tm, d),
        out_shape=jax.ShapeDtypeStruct((bsz, seq, d), F32),
        scratch_shapes=[pltpu.VMEM((SUBLANES, f), F32), pltpu.VMEM((SUBLANES + sub, f_chunk), F32),
                        pltpu.VMEM((n_sub, sub, d), F32), pltpu.VMEM((2, sub, d), F32)],
        compiler_params=_params(),
        name="ffn_ple",
    )(x, p, w_up.astype(BF16), w_gate.astype(BF16), conv_w, row(conv_b), w_down.astype(BF16),
      ple_gate.astype(BF16), ple_proj.astype(BF16), row(ln_g), row(ln_b))


def _kv_body(x_ref, wkt_ref, wv_ref, kt_ref, v_ref):
    tm, d = x_ref.shape[1], x_ref.shape[2]
    xb = x_ref[0].astype(BF16)
    kt = lax.dot_general(wkt_ref[...], xb, (((1,), (1,)), ((), ())),
                         preferred_element_type=F32)
    v = _dot(xb, wv_ref[...])
    for j in range(tm // ATT_BLOCK):
        js = slice(j * ATT_BLOCK, (j + 1) * ATT_BLOCK)
        for p in range(d // LANES):
            ps = slice(p * LANES, (p + 1) * LANES)
            kt_ref[0, j, p] = kt[ps, js].astype(BF16)
            v_ref[0, j, p] = v[js, ps].astype(BF16)


def _kv_project(x, wk, wv, *, tm):
    bsz, seq, d = x.shape
    nkb, npairs = seq // ATT_BLOCK, d // LANES
    blocked = jax.ShapeDtypeStruct((bsz, nkb, npairs, ATT_BLOCK, LANES), BF16)
    spec = pl.BlockSpec((1, tm // ATT_BLOCK, npairs, ATT_BLOCK, LANES),
                        lambda b, s: (b, s, 0, 0, 0))
    return pl.pallas_call(
        _kv_body,
        grid=(bsz, seq // tm),
        in_specs=[_row_tile(tm, d), _resident((d, d)), _resident((d, d))],
        out_specs=[spec, spec],
        out_shape=[blocked, blocked],
        compiler_params=_params(),
        name="kv_project",
    )(x, wk.T.astype(BF16), wv.astype(BF16))


def _attn_body(x_ref, wq_ref, kt_ref, v_ref, wo_ref, mg_ref, mb_ref, o_ref,
               q_scr, out_scr, tri_scr, carry_scr, acc_scr, *, alpha, pair_group):
    tm, d = x_ref.shape[1], x_ref.shape[2]
    npairs = d // LANES
    nqb = tm // ATT_BLOCK
    blk = ATT_BLOCK
    seq_tile = pl.program_id(1)

    x = x_ref[0]
    q = _dot(x.astype(BF16), wq_ref[...])
    for p in range(npairs):
        q_scr[p] = q[:, p * LANES:(p + 1) * LANES].astype(BF16)

    rr = lax.broadcasted_iota(jnp.int32, (2 * blk, 2 * blk), 0) % blk
    cc = lax.broadcasted_iota(jnp.int32, (2 * blk, 2 * blk), 1)
    tri_scr[...] = jnp.where((cc >= blk) | (rr > cc), 1.0, 0.0).astype(BF16)

    lane = lax.broadcasted_iota(jnp.int32, (blk, LANES), 1)
    even_head = lane < HEAD_DIM
    row2 = lax.broadcasted_iota(jnp.int32, (2 * blk, blk), 0) % blk
    col2 = lax.broadcasted_iota(jnp.int32, (2 * blk, blk), 1)
    causal = col2 < row2

    def split_heads(t, axis):
        zero = jnp.zeros_like(t)
        return jnp.concatenate([jnp.where(even_head, t, zero), jnp.where(even_head, zero, t)],
                               axis=axis)

    def sweep_block(j, pairs, qcats, diagonal):
        n = len(pairs)
        zs = [_dot(qcats[g], kt_ref[0, j, pairs[g]]) for g in range(n)]
        log_bs, lcats = [], []
        for z in zs:
            m = jnp.minimum(z, 0.0)
            soft = jnp.log(1.0 + jnp.exp2(m + m - z)) * LOG2_E
            log_b = m - soft
            log_1m = log_b - z
            if diagonal:
                log_1m = jnp.where(causal, log_1m, 0.0)
            hi = log_1m.astype(BF16)
            lo = (log_1m - hi.astype(F32)).astype(BF16)
            log_bs.append(log_b)
            lcats.append(jnp.concatenate([hi, lo], axis=1))
        sums = [_dot(lcat, tri_scr[...]) for lcat in lcats]
        top = None
        acats = []
        for g in range(n):
            carry = carry_scr[g]
            a = jnp.exp2(log_bs[g] + sums[g][:, :blk] + carry)
            if diagonal:
                a = jnp.where(causal, a, 0.0)
            a = a.astype(BF16)
            acats.append(jnp.concatenate([a[:blk], a[blk:]], axis=1))
            carry = carry + sums[g][:, blk:]
            carry_scr[g] = carry
            top = carry if top is None else jnp.maximum(top, carry)
        for g in range(n):
            acc_scr[g] += _dot(acats[g], split_heads(v_ref[0, j, pairs[g]], 0))
        return jnp.max(top)

    for p0 in range(0, npairs, pair_group):
        pairs = tuple(range(p0, p0 + pair_group))

        def per_query_block(qi, c, pairs=pairs):
            r0 = pl.multiple_of(qi * blk, blk)
            qcats = [split_heads(q_scr[p, pl.ds(r0, blk), :], 0) for p in pairs]
            carry_scr[...] = jnp.zeros(carry_scr.shape, F32)
            acc_scr[...] = jnp.zeros(acc_scr.shape, F32)
            jq = seq_tile * nqb + qi
            top = sweep_block(jq, pairs, qcats, True)

            def cond(state):
                j, top = state
                return (j >= 0) & (top >= F32_EXP2_ZERO_BELOW)

            def step(state, pairs=pairs, qcats=qcats):
                j, _ = state
                return j - 1, sweep_block(j, pairs, qcats, False)

            lax.while_loop(cond, step, (jq - 1, top))
            for g, p in enumerate(pairs):
                out_scr[p, pl.ds(r0, blk), :] = acc_scr[g].astype(BF16)
            return c

        lax.fori_loop(0, nqb, per_query_block, 0)

    attn = jnp.concatenate([out_scr[p] for p in range(npairs)], axis=1)
    mix = _dot(attn, wo_ref[...])
    o_ref[0] = _layer_norm(alpha * x + mix, mg_ref[...], mb_ref[...])


def _attention_layer(x, wq, kt, v, wo, mix_g, mix_b, *, alpha, tm):
    bsz, seq, d = x.shape
    nkb, npairs = seq // ATT_BLOCK, d // LANES
    assert seq % tm == 0 and tm % ATT_BLOCK == 0 and d % LANES == 0
    row = lambda a: a.reshape(1, -1)
    kv_spec = pl.BlockSpec((1, nkb, npairs, ATT_BLOCK, LANES), lambda b, s: (b, 0, 0, 0, 0),
                           pipeline_mode=pl.Buffered(1))
    pair_group = math.gcd(npairs, ATT_PAIR_GROUP)
    scale = HEAD_DIM ** -0.5 * LOG2_E
    body = functools.partial(_attn_body, alpha=alpha, pair_group=pair_group)
    return pl.pallas_call(
        body,
        grid=(bsz, seq // tm),
        in_specs=[_row_tile(tm, d), _resident((d, d)), kv_spec, kv_spec, _resident((d, d)),
                  _resident((1, d)), _resident((1, d))],
        out_specs=_row_tile(tm, d),
        out_shape=jax.ShapeDtypeStruct((bsz, seq, d), F32),
        scratch_shapes=[pltpu.VMEM((npairs, tm, LANES), BF16),
                        pltpu.VMEM((npairs, tm, LANES), BF16),
                        pltpu.VMEM((2 * ATT_BLOCK, 2 * ATT_BLOCK), BF16),
                        pltpu.VMEM((pair_group, 2 * ATT_BLOCK, ATT_BLOCK), F32),
                        pltpu.VMEM((pair_group, ATT_BLOCK, LANES), F32)],
        compiler_params=_params(),
        name="stickbreak_attention",
    )(x, (wq * scale).astype(BF16), kt, v, wo.astype(BF16), row(mix_g), row(mix_b))


def kernel(x, p, a_pw1_w, a_pw1_b, a_dw_w, a_dw_b, a_ln_g, a_ln_b, a_pw2_w, a_pw2_b, b_wq, kv_wk, kv_wv, b_wo, ln_mix_g, ln_mix_b, ffn_w_up, ffn_w_gate, ffn_conv_w, ffn_conv_b, ffn_w_down, ple_w_gate, ple_w_proj, ln_ffn_g, ln_ffn_b):
    depth = p.shape[0]
    n_conf = a_pw1_w.shape[0]
    seq = x.shape[1]
    alpha = (2.0 * depth) ** 0.25
    tm = min(512, seq)
    ffn_tm = 2 * tm if seq % (2 * tm) == 0 else tm
    f = ffn_w_up.shape[-1]
    f_chunk = 256 if f % 256 == 0 else LANES
    kt = v = None
    for i in range(depth):
        if i < n_conf:
            x = _conformer_layer(x, a_pw1_w[i], a_pw1_b[i], a_dw_w[i], a_dw_b[i], a_ln_g[i],
                                 a_ln_b[i], a_pw2_w[i], a_pw2_b[i], ln_mix_g[i], ln_mix_b[i],
                                 alpha=alpha, tm=tm)
        else:
            j = i - n_conf
            if kt is None:
                kt, v = _kv_project(x, kv_wk, kv_wv, tm=tm)
            x = _attention_layer(x, b_wq[j], kt, v, b_wo[j], ln_mix_g[i], ln_mix_b[i],
                                 alpha=alpha, tm=tm)
        x = _ffn_layer(x, p, i, ffn_w_up[i], ffn_w_gate[i], ffn_conv_w[i], ffn_conv_b[i],
                       ffn_w_down[i], ple_w_gate[i], ple_w_proj[i], ln_ffn_g[i], ln_ffn_b[i],
                       alpha=alpha, tm=ffn_tm, n_sub=ffn_tm // tm, f_chunk=f_chunk)
    return x
```

```python
import functools
import math

import jax
import jax.numpy as jnp
from jax import lax
from jax.experimental import pallas as pl
from jax.experimental.pallas import tpu as pltpu

F32 = jnp.float32
BF16 = jnp.bfloat16

LN_EPS = 1e-5
HEAD_DIM = 64
LANES = 128
SUBLANES = 8
ATT_BLOCK = 128
CONV_HALO = 32
LOG2_E = 1.4426950408889634
F32_EXP2_ZERO_BELOW = -151.0
CONFORMER_SUB_TILES = 2
FFN_LOOKAHEAD = 2
ATT_PAIR_GROUP = 8
VMEM_LIMIT_BYTES = 56 * 1024 * 1024


def _layer_norm(v, g, b):
    mu = jnp.mean(v, axis=-1, keepdims=True)
    c = v - mu
    var = jnp.mean(c * c, axis=-1, keepdims=True)
    return c * lax.rsqrt(var + LN_EPS) * g + b


def _dot(a, b):
    return jnp.dot(a, b, preferred_element_type=F32)


def _resident(shape):
    zeros = (0,) * len(shape)
    return pl.BlockSpec(shape, lambda b, s: zeros, pipeline_mode=pl.Buffered(1))


def _row_tile(tm, d):
    return pl.BlockSpec((1, tm, d), lambda b, s: (b, s, 0))


def _params():
    return pltpu.CompilerParams(dimension_semantics=("arbitrary", "arbitrary"),
                                vmem_limit_bytes=VMEM_LIMIT_BYTES)


def _conformer_body(x_ref, pw1_ref, pw1b_ref, dww_ref, dwb_ref, lng_ref, lnb_ref, pw2_ref,
                    pw2b_ref, mg_ref, mb_ref, o_ref, hbuf_ref, cbuf_ref, *, alpha, n_sub):
    tm, d = x_ref.shape[1], x_ref.shape[2]
    nslab, conv_w = dww_ref.shape[0], dww_ref.shape[1]
    sub = tm // n_sub

    @pl.when(pl.program_id(1) == 0)
    def _():
        hbuf_ref[:, 0:CONV_HALO, :] = jnp.zeros((nslab, CONV_HALO, LANES), F32)

    def rows_of(t):
        return slice(t * sub, (t + 1) * sub)

    def pointwise_in(t):
        return _dot(x_ref[0, rows_of(t), :].astype(BF16), pw1_ref[...]) + pw1b_ref[...]

    def glu_to_history(t, h2):
        h = h2[:, :d] * jax.nn.sigmoid(h2[:, d:])
        for s in range(nslab):
            hbuf_ref[s, CONV_HALO + t * sub:CONV_HALO + (t + 1) * sub, :] = \
                h[:, s * LANES:(s + 1) * LANES]

    def conv(t):
        for s in range(nslab):
            taps = [jnp.broadcast_to(dww_ref[s, k:k + 1, :], (SUBLANES, LANES))
                    for k in range(conv_w)]
            bias = jnp.broadcast_to(dwb_ref[s], (SUBLANES, LANES))
            for base, stride in _strided_row_passes(sub):
                for v in range(t * sub + base, t * sub + base + stride):
                    acc = [bias, None]
                    for k in range(conv_w):
                        start = v + (CONV_HALO - (conv_w - 1) + k)
                        term = hbuf_ref[s, pl.ds(start, SUBLANES, stride=stride), :] * taps[k]
                        acc[k % 2] = term if acc[k % 2] is None else acc[k % 2] + term
                    cbuf_ref[s, pl.ds(v, SUBLANES, stride=stride), :] = acc[0] + acc[1]

    def pointwise_out(t):
        c = jnp.concatenate([cbuf_ref[s, rows_of(t), :] for s in range(nslab)], axis=1)
        y = _layer_norm(c, lng_ref[...], lnb_ref[...])
        y = y * jax.nn.sigmoid(y)
        return _dot(y.astype(BF16), pw2_ref[...]) + pw2b_ref[...]

    def finish(t, mix):
        o_ref[0, rows_of(t), :] = _layer_norm(alpha * x_ref[0, rows_of(t), :] + mix,
                                              mg_ref[...], mb_ref[...])

    glu_to_history(0, pointwise_in(0))
    mix = None
    for t in range(n_sub):
        h2_next = pointwise_in(t + 1) if t + 1 < n_sub else None
        conv(t)
        if mix is not None:
            finish(t - 1, mix)
        if h2_next is not None:
            glu_to_history(t + 1, h2_next)
        mix = pointwise_out(t)
    finish(n_sub - 1, mix)
    hbuf_ref[:, 0:CONV_HALO, :] = hbuf_ref[:, tm:tm + CONV_HALO, :]


def _strided_row_passes(n_rows):
    n = n_rows // SUBLANES
    if n % 8 != 0:
        return [(0, n)]
    return [(0, n - 4), (SUBLANES * (n - 4), 4)]


def _conformer_layer(x, pw1_w, pw1_b, dw_w, dw_b, ln_g, ln_b, pw2_w, pw2_b, mix_g, mix_b, *,
                     alpha, tm, n_sub):
    bsz, seq, d = x.shape
    conv_w = dw_w.shape[0]
    assert conv_w - 1 <= CONV_HALO <= tm and seq % tm == 0 and d % LANES == 0
    assert tm % (n_sub * SUBLANES) == 0
    nslab = d // LANES
    row = lambda v: v.reshape(1, -1)
    slabs = lambda v: v.reshape(-1, nslab, LANES).transpose(1, 0, 2)
    body = functools.partial(_conformer_body, alpha=alpha, n_sub=n_sub)
    return pl.pallas_call(
        body,
        grid=(bsz, seq // tm),
        in_specs=[_row_tile(tm, d), _resident((d, 2 * d)), _resident((1, 2 * d)),
                  _resident((nslab, conv_w, LANES)), _resident((nslab, 1, LANES)),
                  _resident((1, d)), _resident((1, d)),
                  _resident((d, d)), _resident((1, d)), _resident((1, d)), _resident((1, d))],
        out_specs=_row_tile(tm, d),
        out_shape=jax.ShapeDtypeStruct((bsz, seq, d), F32),
        scratch_shapes=[pltpu.VMEM((nslab, CONV_HALO + tm, LANES), F32),
                        pltpu.VMEM((nslab, tm, LANES), F32)],
        compiler_params=_params(),
        name="conformer_mixer",
    )(x, pw1_w.astype(BF16), row(pw1_b), slabs(dw_w), slabs(dw_b), row(ln_g), row(ln_b),
      pw2_w.astype(BF16), row(pw2_b), row(mix_g), row(mix_b))


def _ffn_body(x_ref, p_ref, wup_ref, wgate_ref, cw_ref, cb_ref, wdown_ref, pg_ref, pp_ref,
              g_ref, b_ref, o_ref, gprev_ref, gbuf_ref, acc_ref, ple_ref, *, alpha, f_chunk):
    n_sub, sub = acc_ref.shape[0], acc_ref.shape[1]
    f = wup_ref.shape[1]
    conv_w = cw_ref.shape[0]
    nchunks = f // f_chunk

    @pl.when(pl.program_id(1) == 0)
    def _():
        gprev_ref[...] = jnp.zeros(gprev_ref.shape, F32)

    n_slices = min(nchunks - 1, 8) if sub % 64 == 0 else 1
    slice_rows = sub // n_slices

    def finish_rows(t, i):
        rs = slice(i * slice_rows, (i + 1) * slice_rows)
        out_rows = slice(t * sub + i * slice_rows, t * sub + (i + 1) * slice_rows)
        gate = jax.nn.sigmoid(ple_ref[0, rs, :])
        y = alpha * x_ref[0, out_rows, :] + acc_ref[t, rs, :] + gate * ple_ref[1, rs, :]
        o_ref[0, out_rows, :] = _layer_norm(y, g_ref[...], b_ref[...])

    for t in range(n_sub):
        rows = slice(t * sub, (t + 1) * sub)
        xb = x_ref[0, rows, :].astype(BF16)

        def up_gate(c, xb=xb):
            fs = slice(c * f_chunk, (c + 1) * f_chunk)
            return _dot(xb, wup_ref[:, fs]), _dot(xb, wgate_ref[:, fs])

        queue = [up_gate(c) for c in range(min(FFN_LOOKAHEAD, nchunks))]
        for c in range(nchunks):
            fs = slice(c * f_chunk, (c + 1) * f_chunk)
            if c + FFN_LOOKAHEAD < nchunks:
                queue.append(up_gate(c + FFN_LOOKAHEAD))
            u, g = queue.pop(0)
            if t > 0 and c < n_slices:
                finish_rows(t - 1, c)
            if c + 1 == nchunks:
                ple_ref[0] = _dot(xb, pg_ref[...])
                ple_ref[1] = _dot(p_ref[0, 0, rows, :].astype(BF16), pp_ref[...])
            gbuf_ref[0:SUBLANES, :] = gprev_ref[:, fs]
            gbuf_ref[SUBLANES:SUBLANES + sub, :] = g
            gprev_ref[:, fs] = g[sub - SUBLANES:sub, :]
            gc = cb_ref[:, fs] + cw_ref[conv_w - 1:conv_w, fs] * g
            for k in range(conv_w - 1):
                off = SUBLANES - (conv_w - 1) + k
                gc = gc + cw_ref[k:k + 1, fs] * gbuf_ref[off:off + sub, :]
            hidden = (gc * jax.nn.sigmoid(gc) * u).astype(BF16)
            part = _dot(hidden, wdown_ref[fs, :])
            if c == 0:
                acc_ref[t] = part
            else:
                acc_ref[t] += part
    for i in range(n_slices):
        finish_rows(n_sub - 1, i)


def _ffn_layer(x, p, layer, w_up, w_gate, conv_w, conv_b, w_down, ple_gate, ple_proj, ln_g,
               ln_b, *, alpha, tm, n_sub, f_chunk):
    bsz, seq, d = x.shape
    f = w_up.shape[1]
    pd = p.shape[-1]
    kw = conv_w.shape[0]
    sub = tm // n_sub
    assert kw - 1 <= SUBLANES <= sub and seq % tm == 0 and tm % n_sub == 0 and f % f_chunk == 0
    row = lambda v: v.reshape(1, -1)
    body = functools.partial(_ffn_body, alpha=alpha, f_chunk=f_chunk)
    p_spec = pl.BlockSpec((1, 1, tm, pd), lambda b, s: (layer, b, s, 0))
    return pl.pallas_call(
        body,
        grid=(bsz, seq // tm),
        in_specs=[_row_tile(tm, d), p_spec, _resident((d, f)), _resident((d, f)),
                  _resident((kw, f)), _resident((1, f)), _resident((f, d)), _resident((d, d)),
                  _resident((pd, d)), _resident((1, d)), _resident((1, d))],
        out_specs=_row_tile(tm, d),
        out_shape=jax.ShapeDtypeStruct((bsz, seq, d), F32),
        scratch_shapes=[pltpu.VMEM((SUBLANES, f), F32), pltpu.VMEM((SUBLANES + sub, f_chunk), F32),
                        pltpu.VMEM((n_sub, sub, d), F32), pltpu.VMEM((2, sub, d), F32)],
        compiler_params=_params(),
        name="ffn_ple",
    )(x, p, w_up.astype(BF16), w_gate.astype(BF16), conv_w, row(conv_b), w_down.astype(BF16),
      ple_gate.astype(BF16), ple_proj.astype(BF16), row(ln_g), row(ln_b))


def _kv_body(x_ref, wkt_ref, wv_ref, kt_ref, v_ref):
    tm, d = x_ref.shape[1], x_ref.shape[2]
    xb = x_ref[0].astype(BF16)
    kt = lax.dot_general(wkt_ref[...], xb, (((1,), (1,)), ((), ())),
                         preferred_element_type=F32)
    v = _dot(xb, wv_ref[...])
    for j in range(tm // ATT_BLOCK):
        js = slice(j * ATT_BLOCK, (j + 1) * ATT_BLOCK)
        for p in range(d // LANES):
            ps = slice(p * LANES, (p + 1) * LANES)
            kt_ref[0, j, p] = kt[ps, js].astype(BF16)
            v_ref[0, j, p] = v[js, ps].astype(BF16)


def _kv_project(x, wk, wv, *, tm):
    bsz, seq, d = x.shape
    nkb, npairs = seq // ATT_BLOCK, d // LANES
    blocked = jax.ShapeDtypeStruct((bsz, nkb, npairs, ATT_BLOCK, LANES), BF16)
    spec = pl.BlockSpec((1, tm // ATT_BLOCK, npairs, ATT_BLOCK, LANES),
                        lambda b, s: (b, s, 0, 0, 0))
    return pl.pallas_call(
        _kv_body,
        grid=(bsz, seq // tm),
        in_specs=[_row_tile(tm, d), _resident((d, d)), _resident((d, d))],
        out_specs=[spec, spec],
        out_shape=[blocked, blocked],
        compiler_params=_params(),
        name="kv_project",
    )(x, wk.T.astype(BF16), wv.astype(BF16))


def _attn_body(x_ref, wq_ref, kt_ref, v_ref, wo_ref, mg_ref, mb_ref, o_ref,
               q_scr, out_scr, tri_scr, carry_scr, acc_scr, *, alpha, pair_group):
    tm, d = x_ref.shape[1], x_ref.shape[2]
    npairs = d // LANES
    nqb = tm // ATT_BLOCK
    blk = ATT_BLOCK
    seq_tile = pl.program_id(1)

    def rows_of(qi):
        return slice(qi * blk, (qi + 1) * blk)

    def project_q(qi):
        q = _dot(x_ref[0, rows_of(qi), :].astype(BF16), wq_ref[...])
        for p in range(npairs):
            q_scr[p, rows_of(qi), :] = q[:, p * LANES:(p + 1) * LANES].astype(BF16)

    def project_out(qi):
        attn = jnp.concatenate([out_scr[p, rows_of(qi), :] for p in range(npairs)], axis=1)
        mix = _dot(attn, wo_ref[...])
        o_ref[0, rows_of(qi), :] = _layer_norm(alpha * x_ref[0, rows_of(qi), :] + mix,
                                               mg_ref[...], mb_ref[...])

    rr = lax.broadcasted_iota(jnp.int32, (2 * blk, 2 * blk), 0) % blk
    cc = lax.broadcasted_iota(jnp.int32, (2 * blk, 2 * blk), 1)
    tri_scr[...] = jnp.where((cc >= blk) | (rr > cc), 1.0, 0.0).astype(BF16)

    lane = lax.broadcasted_iota(jnp.int32, (blk, LANES), 1)
    even_head = lane < HEAD_DIM
    row2 = lax.broadcasted_iota(jnp.int32, (2 * blk, blk), 0) % blk
    col2 = lax.broadcasted_iota(jnp.int32, (2 * blk, blk), 1)
    causal = col2 < row2

    def split_heads(t, axis):
        zero = jnp.zeros_like(t)
        return jnp.concatenate([jnp.where(even_head, t, zero), jnp.where(even_head, zero, t)],
                               axis=axis)

    def sweep(blocks, pairs, qcats, fresh, between=None):
        n = len(pairs)
        zs = [[_dot(qcats[g], kt_ref[0, j, pairs[g]]) for j, _ in blocks] for g in range(n)]
        if between is not None:
            between()
        log_bs, lcats = [], []
        for g in range(n):
            log_bs.append([])
            lcats.append([])
            for z, (_, diagonal) in zip(zs[g], blocks):
                m = jnp.minimum(z, 0.0)
                soft = jnp.log(1.0 + jnp.exp2(m + m - z)) * LOG2_E
                log_b = m - soft
                log_1m = log_b - z
                if diagonal:
                    log_1m = jnp.where(causal, log_1m, 0.0)
                hi = log_1m.astype(BF16)
                lo = (log_1m - hi.astype(F32)).astype(BF16)
                log_bs[g].append(log_b)
                lcats[g].append(jnp.concatenate([hi, lo], axis=1))
        sums = [[_dot(lcat, tri_scr[...]) for lcat in lcats[g]] for g in range(n)]
        top = None
        acats = []
        for g in range(n):
            carry = None if fresh else carry_scr[g]
            acats.append([])
            for b, (_, diagonal) in enumerate(blocks):
                log_a = log_bs[g][b] + sums[g][b][:, :blk]
                if carry is not None:
                    log_a = log_a + carry
                a = jnp.exp2(log_a)
                if diagonal:
                    a = jnp.where(causal, a, 0.0)
                a = a.astype(BF16)
                acats[g].append(jnp.concatenate([a[:blk], a[blk:]], axis=1))
                total = sums[g][b][:, blk:]
                carry = total if carry is None else carry + total
            carry_scr[g] = carry
            top = carry if top is None else jnp.maximum(top, carry)
        for g in range(n):
            acc = None if fresh else acc_scr[g]
            for b, (j, _) in enumerate(blocks):
                part = _dot(acats[g][b], split_heads(v_ref[0, j, pairs[g]], 0))
                acc = part if acc is None else acc + part
            acc_scr[g] = acc
        return jnp.max(top)

    project_q(0)
    for qi in range(nqb):
        jq = seq_tile * nqb + qi
        for p0 in range(0, npairs, pair_group):
            pairs = tuple(range(p0, p0 + pair_group))
            qcats = [split_heads(q_scr[p, rows_of(qi), :], 0) for p in pairs]

            def neighbours(qi=qi):
                if qi + 1 < nqb:
                    project_q(qi + 1)
                if qi > 0:
                    project_out(qi - 1)

            between = neighbours if p0 == 0 else None
            if qi == 0:
                top = sweep([(jq, True)], pairs, qcats, True, between)
                j_next = jq - 1
            else:
                top = sweep([(jq, True), (jq - 1, False)], pairs, qcats, True, between)
                j_next = jq - 2

            def cond(state):
                j, top = state
                return (j >= 0) & (top >= F32_EXP2_ZERO_BELOW)

            def step(state, pairs=pairs, qcats=qcats):
                j, _ = state
                return j - 1, sweep([(j, False)], pairs, qcats, False)

            lax.while_loop(cond, step, (j_next, top))
            for g, p in enumerate(pairs):
                out_scr[p, rows_of(qi), :] = acc_scr[g].astype(BF16)
    project_out(nqb - 1)


def _attention_layer(x, wq, kt, v, wo, mix_g, mix_b, *, alpha, tm):
    bsz, seq, d = x.shape
    nkb, npairs = seq // ATT_BLOCK, d // LANES
    assert seq % tm == 0 and tm % ATT_BLOCK == 0 and d % LANES == 0
    row = lambda a: a.reshape(1, -1)
    kv_spec = pl.BlockSpec((1, nkb, npairs, ATT_BLOCK, LANES), lambda b, s: (b, 0, 0, 0, 0),
                           pipeline_mode=pl.Buffered(1))
    pair_group = math.gcd(npairs, ATT_PAIR_GROUP)
    scale = HEAD_DIM ** -0.5 * LOG2_E
    body = functools.partial(_attn_body, alpha=alpha, pair_group=pair_group)
    return pl.pallas_call(
        body,
        grid=(bsz, seq // tm),
        in_specs=[_row_tile(tm, d), _resident((d, d)), kv_spec, kv_spec, _resident((d, d)),
                  _resident((1, d)), _resident((1, d))],
        out_specs=_row_tile(tm, d),
        out_shape=jax.ShapeDtypeStruct((bsz, seq, d), F32),
        scratch_shapes=[pltpu.VMEM((npairs, tm, LANES), BF16),
                        pltpu.VMEM((npairs, tm, LANES), BF16),
                        pltpu.VMEM((2 * ATT_BLOCK, 2 * ATT_BLOCK), BF16),
                        pltpu.VMEM((pair_group, 2 * ATT_BLOCK, ATT_BLOCK), F32),
                        pltpu.VMEM((pair_group, ATT_BLOCK, LANES), F32)],
        compiler_params=_params(),
        name="stickbreak_attention",
    )(x, (wq * scale).astype(BF16), kt, v, wo.astype(BF16), row(mix_g), row(mix_b))


def kernel(x, p, a_pw1_w, a_pw1_b, a_dw_w, a_dw_b, a_ln_g, a_ln_b, a_pw2_w, a_pw2_b, b_wq, kv_wk, kv_wv, b_wo, ln_mix_g, ln_mix_b, ffn_w_up, ffn_w_gate, ffn_conv_w, ffn_conv_b, ffn_w_down, ple_w_gate, ple_w_proj, ln_ffn_g, ln_ffn_b):
    depth = p.shape[0]
    n_conf = a_pw1_w.shape[0]
    seq = x.shape[1]
    alpha = (2.0 * depth) ** 0.25
    tm = min(512, seq)
    ffn_tm = 2 * tm if seq % (2 * tm) == 0 else tm
    f = ffn_w_up.shape[-1]
    f_chunk = 256 if f % 256 == 0 else LANES
    kt = v = None
    for i in range(depth):
        if i < n_conf:
            x = _conformer_layer(x, a_pw1_w[i], a_pw1_b[i], a_dw_w[i], a_dw_b[i], a_ln_g[i],
                                 a_ln_b[i], a_pw2_w[i], a_pw2_b[i], ln_mix_g[i], ln_mix_b[i],
                                 alpha=alpha, tm=ffn_tm, n_sub=CONFORMER_SUB_TILES)
        else:
            j = i - n_conf
            if kt is None:
                kt, v = _kv_project(x, kv_wk, kv_wv, tm=tm)
            x = _attention_layer(x, b_wq[j], kt, v, b_wo[j], ln_mix_g[i], ln_mix_b[i],
                                 alpha=alpha, tm=tm)
        x = _ffn_layer(x, p, i, ffn_w_up[i], ffn_w_gate[i], ffn_conv_w[i], ffn_conv_b[i],
                       ffn_w_down[i], ple_w_gate[i], ple_w_proj[i], ln_ffn_g[i], ln_ffn_b[i],
                       alpha=alpha, tm=ffn_tm, n_sub=ffn_tm // tm, f_chunk=f_chunk)
    return x
```

```python
import functools
import math

import jax
import jax.numpy as jnp
from jax import lax
from jax.experimental import pallas as pl
from jax.experimental.pallas import tpu as pltpu

F32 = jnp.float32
BF16 = jnp.bfloat16

LN_EPS = 1e-5
HEAD_DIM = 64
LANES = 128
SUBLANES = 8
ATT_BLOCK = 128
CONV_HALO = 32
LOG2_E = 1.4426950408889634
F32_EXP2_ZERO_BELOW = -151.0
CONFORMER_SUB_TILES = 2
FFN_LOOKAHEAD = 3
ATT_PAIR_GROUP = 8
VMEM_LIMIT_BYTES = 56 * 1024 * 1024


def _layer_norm(v, g, b):
    mu = jnp.mean(v, axis=-1, keepdims=True)
    c = v - mu
    var = jnp.mean(c * c, axis=-1, keepdims=True)
    return c * lax.rsqrt(var + LN_EPS) * g + b


def _dot(a, b):
    return jnp.dot(a, b, preferred_element_type=F32)


def _resident(shape):
    zeros = (0,) * len(shape)
    return pl.BlockSpec(shape, lambda b, s: zeros, pipeline_mode=pl.Buffered(1))


def _row_tile(tm, d):
    return pl.BlockSpec((1, tm, d), lambda b, s: (b, s, 0))


def _params():
    return pltpu.CompilerParams(dimension_semantics=("arbitrary", "arbitrary"),
                                vmem_limit_bytes=VMEM_LIMIT_BYTES)


def _conformer_body(x_ref, pw1_ref, pw1b_ref, dww_ref, dwb_ref, lng_ref, lnb_ref, pw2_ref,
                    pw2b_ref, mg_ref, mb_ref, o_ref, hbuf_ref, cbuf_ref, *, alpha, n_sub):
    tm, d = x_ref.shape[1], x_ref.shape[2]
    nslab, conv_w = dww_ref.shape[0], dww_ref.shape[1]
    sub = tm // n_sub

    @pl.when(pl.program_id(1) == 0)
    def _():
        hbuf_ref[:, 0:CONV_HALO, :] = jnp.zeros((nslab, CONV_HALO, LANES), F32)

    def rows_of(t):
        return slice(t * sub, (t + 1) * sub)

    def pointwise_in(t):
        return _dot(x_ref[0, rows_of(t), :].astype(BF16), pw1_ref[...]) + pw1b_ref[...]

    def glu_to_history(t, h2):
        h = h2[:, :d] * jax.nn.sigmoid(h2[:, d:])
        for s in range(nslab):
            hbuf_ref[s, CONV_HALO + t * sub:CONV_HALO + (t + 1) * sub, :] = \
                h[:, s * LANES:(s + 1) * LANES]

    def conv(t):
        for s in range(nslab):
            taps = [jnp.broadcast_to(dww_ref[s, k:k + 1, :], (SUBLANES, LANES))
                    for k in range(conv_w)]
            bias = jnp.broadcast_to(dwb_ref[s], (SUBLANES, LANES))
            for base, stride in _strided_row_passes(sub):
                for v in range(t * sub + base, t * sub + base + stride):
                    acc = [bias, None]
                    for k in range(conv_w):
                        start = v + (CONV_HALO - (conv_w - 1) + k)
                        term = hbuf_ref[s, pl.ds(start, SUBLANES, stride=stride), :] * taps[k]
                        acc[k % 2] = term if acc[k % 2] is None else acc[k % 2] + term
                    cbuf_ref[s, pl.ds(v, SUBLANES, stride=stride), :] = acc[0] + acc[1]

    def pointwise_out(t):
        c = jnp.concatenate([cbuf_ref[s, rows_of(t), :] for s in range(nslab)], axis=1)
        y = _layer_norm(c, lng_ref[...], lnb_ref[...])
        y = y * jax.nn.sigmoid(y)
        return _dot(y.astype(BF16), pw2_ref[...]) + pw2b_ref[...]

    def finish(t, mix):
        o_ref[0, rows_of(t), :] = _layer_norm(alpha * x_ref[0, rows_of(t), :] + mix,
                                              mg_ref[...], mb_ref[...])

    glu_to_history(0, pointwise_in(0))
    mix = None
    for t in range(n_sub):
        h2_next = pointwise_in(t + 1) if t + 1 < n_sub else None
        conv(t)
        if mix is not None:
            finish(t - 1, mix)
        if h2_next is not None:
            glu_to_history(t + 1, h2_next)
        mix = pointwise_out(t)
    finish(n_sub - 1, mix)
    hbuf_ref[:, 0:CONV_HALO, :] = hbuf_ref[:, tm:tm + CONV_HALO, :]


def _strided_row_passes(n_rows):
    n = n_rows // SUBLANES
    if n % 8 != 0:
        return [(0, n)]
    return [(0, n - 4), (SUBLANES * (n - 4), 4)]


def _conformer_layer(x, pw1_w, pw1_b, dw_w, dw_b, ln_g, ln_b, pw2_w, pw2_b, mix_g, mix_b, *,
                     alpha, tm, n_sub):
    bsz, seq, d = x.shape
    conv_w = dw_w.shape[0]
    assert conv_w - 1 <= CONV_HALO <= tm and seq % tm == 0 and d % LANES == 0
    assert tm % (n_sub * SUBLANES) == 0
    nslab = d // LANES
    row = lambda v: v.reshape(1, -1)
    slabs = lambda v: v.reshape(-1, nslab, LANES).transpose(1, 0, 2)
    body = functools.partial(_conformer_body, alpha=alpha, n_sub=n_sub)
    return pl.pallas_call(
        body,
        grid=(bsz, seq // tm),
        in_specs=[_row_tile(tm, d), _resident((d, 2 * d)), _resident((1, 2 * d)),
                  _resident((nslab, conv_w, LANES)), _resident((nslab, 1, LANES)),
                  _resident((1, d)), _resident((1, d)),
                  _resident((d, d)), _resident((1, d)), _resident((1, d)), _resident((1, d))],
        out_specs=_row_tile(tm, d),
        out_shape=jax.ShapeDtypeStruct((bsz, seq, d), F32),
        scratch_shapes=[pltpu.VMEM((nslab, CONV_HALO + tm, LANES), F32),
                        pltpu.VMEM((nslab, tm, LANES), F32)],
        compiler_params=_params(),
        name="conformer_mixer",
    )(x, pw1_w.astype(BF16), row(pw1_b), slabs(dw_w), slabs(dw_b), row(ln_g), row(ln_b),
      pw2_w.astype(BF16), row(pw2_b), row(mix_g), row(mix_b))


def _ffn_body(x_ref, p_ref, wup_ref, wgate_ref, cw_ref, cb_ref, wdown_ref, pg_ref, pp_ref,
              g_ref, b_ref, o_ref, gprev_ref, gbuf_ref, acc_ref, ple_ref, *, alpha, f_chunk):
    n_sub, sub = acc_ref.shape[0], acc_ref.shape[1]
    f = wup_ref.shape[1]
    conv_w = cw_ref.shape[0]
    nchunks = f // f_chunk

    @pl.when(pl.program_id(1) == 0)
    def _():
        gprev_ref[...] = jnp.zeros(gprev_ref.shape, F32)

    n_slices = min(nchunks - 1, 8) if sub % 64 == 0 else 1
    slice_rows = sub // n_slices

    def finish_rows(t, i):
        rs = slice(i * slice_rows, (i + 1) * slice_rows)
        out_rows = slice(t * sub + i * slice_rows, t * sub + (i + 1) * slice_rows)
        gate = jax.nn.sigmoid(ple_ref[0, rs, :])
        y = alpha * x_ref[0, out_rows, :] + acc_ref[t, rs, :] + gate * ple_ref[1, rs, :]
        o_ref[0, out_rows, :] = _layer_norm(y, g_ref[...], b_ref[...])

    for t in range(n_sub):
        rows = slice(t * sub, (t + 1) * sub)
        xb = x_ref[0, rows, :].astype(BF16)

        def up_gate(c, xb=xb):
            fs = slice(c * f_chunk, (c + 1) * f_chunk)
            return _dot(xb, wup_ref[:, fs]), _dot(xb, wgate_ref[:, fs])

        queue = [up_gate(c) for c in range(min(FFN_LOOKAHEAD, nchunks))]
        for c in range(nchunks):
            fs = slice(c * f_chunk, (c + 1) * f_chunk)
            if c + FFN_LOOKAHEAD < nchunks:
                queue.append(up_gate(c + FFN_LOOKAHEAD))
            u, g = queue.pop(0)
            if t > 0 and c < n_slices:
                finish_rows(t - 1, c)
            if c + 1 == nchunks:
                ple_ref[0] = _dot(xb, pg_ref[...])
                ple_ref[1] = _dot(p_ref[0, 0, rows, :].astype(BF16), pp_ref[...])
            gbuf_ref[0:SUBLANES, :] = gprev_ref[:, fs]
            gbuf_ref[SUBLANES:SUBLANES + sub, :] = g
            gprev_ref[:, fs] = g[sub - SUBLANES:sub, :]
            gc = cb_ref[:, fs] + cw_ref[conv_w - 1:conv_w, fs] * g
            for k in range(conv_w - 1):
                off = SUBLANES - (conv_w - 1) + k
                gc = gc + cw_ref[k:k + 1, fs] * gbuf_ref[off:off + sub, :]
            hidden = (gc * jax.nn.sigmoid(gc) * u).astype(BF16)
            part = _dot(hidden, wdown_ref[fs, :])
            if c == 0:
                acc_ref[t] = part
            else:
                acc_ref[t] += part
    for i in range(n_slices):
        finish_rows(n_sub - 1, i)


def _ffn_layer(x, p, layer, w_up, w_gate, conv_w, conv_b, w_down, ple_gate, ple_proj, ln_g,
               ln_b, *, alpha, tm, n_sub, f_chunk):
    bsz, seq, d = x.shape
    f = w_up.shape[1]
    pd = p.shape[-1]
    kw = conv_w.shape[0]
    sub = tm // n_sub
    assert kw - 1 <= SUBLANES <= sub and seq % tm == 0 and tm % n_sub == 0 and f % f_chunk == 0
    row = lambda v: v.reshape(1, -1)
    body = functools.partial(_ffn_body, alpha=alpha, f_chunk=f_chunk)
    p_spec = pl.BlockSpec((1, 1, tm, pd), lambda b, s: (layer, b, s, 0))
    return pl.pallas_call(
        body,
        grid=(bsz, seq // tm),
        in_specs=[_row_tile(tm, d), p_spec, _resident((d, f)), _resident((d, f)),
                  _resident((kw, f)), _resident((1, f)), _resident((f, d)), _resident((d, d)),
                  _resident((pd, d)), _resident((1, d)), _resident((1, d))],
        out_specs=_row_tile(tm, d),
        out_shape=jax.ShapeDtypeStruct((bsz, seq, d), F32),
        scratch_shapes=[pltpu.VMEM((SUBLANES, f), F32), pltpu.VMEM((SUBLANES + sub, f_chunk), F32),
                        pltpu.VMEM((n_sub, sub, d), F32), pltpu.VMEM((2, sub, d), F32)],
        compiler_params=_params(),
        name="ffn_ple",
    )(x, p, w_up.astype(BF16), w_gate.astype(BF16), conv_w, row(conv_b), w_down.astype(BF16),
      ple_gate.astype(BF16), ple_proj.astype(BF16), row(ln_g), row(ln_b))


def _kv_body(x_ref, wkt_ref, wv_ref, kt_ref, v_ref):
    tm, d = x_ref.shape[1], x_ref.shape[2]
    xb = x_ref[0].astype(BF16)
    kt = lax.dot_general(wkt_ref[...], xb, (((1,), (1,)), ((), ())),
                         preferred_element_type=F32)
    v = _dot(xb, wv_ref[...])
    for j in range(tm // ATT_BLOCK):
        js = slice(j * ATT_BLOCK, (j + 1) * ATT_BLOCK)
        for p in range(d // LANES):
            ps = slice(p * LANES, (p + 1) * LANES)
            kt_ref[0, j, p] = kt[ps, js].astype(BF16)
            v_ref[0, j, p] = v[js, ps].astype(BF16)


def _kv_project(x, wk, wv, *, tm):
    bsz, seq, d = x.shape
    nkb, npairs = seq // ATT_BLOCK, d // LANES
    blocked = jax.ShapeDtypeStruct((bsz, nkb, npairs, ATT_BLOCK, LANES), BF16)
    spec = pl.BlockSpec((1, tm // ATT_BLOCK, npairs, ATT_BLOCK, LANES),
                        lambda b, s: (b, s, 0, 0, 0))
    return pl.pallas_call(
        _kv_body,
        grid=(bsz, seq // tm),
        in_specs=[_row_tile(tm, d), _resident((d, d)), _resident((d, d))],
        out_specs=[spec, spec],
        out_shape=[blocked, blocked],
        compiler_params=_params(),
        name="kv_project",
    )(x, wk.T.astype(BF16), wv.astype(BF16))


def _attn_body(x_ref, wq_ref, kt_ref, v_ref, wo_ref, mg_ref, mb_ref, o_ref,
               q_scr, out_scr, tri_scr, carry_scr, acc_scr, *, alpha, pair_group):
    tm, d = x_ref.shape[1], x_ref.shape[2]
    npairs = d // LANES
    nqb = tm // ATT_BLOCK
    blk = ATT_BLOCK
    seq_tile = pl.program_id(1)

    def rows_of(qi):
        return slice(qi * blk, (qi + 1) * blk)

    def project_q(qi):
        q = _dot(x_ref[0, rows_of(qi), :].astype(BF16), wq_ref[...])
        for p in range(npairs):
            q_scr[p, rows_of(qi), :] = q[:, p * LANES:(p + 1) * LANES].astype(BF16)

    def project_out(qi):
        attn = jnp.concatenate([out_scr[p, rows_of(qi), :] for p in range(npairs)], axis=1)
        mix = _dot(attn, wo_ref[...])
        o_ref[0, rows_of(qi), :] = _layer_norm(alpha * x_ref[0, rows_of(qi), :] + mix,
                                               mg_ref[...], mb_ref[...])

    rr = lax.broadcasted_iota(jnp.int32, (2 * blk, 2 * blk), 0) % blk
    cc = lax.broadcasted_iota(jnp.int32, (2 * blk, 2 * blk), 1)
    tri_scr[...] = jnp.where((cc >= blk) | (rr > cc), 1.0, 0.0).astype(BF16)

    lane = lax.broadcasted_iota(jnp.int32, (blk, LANES), 1)
    even_head = lane < HEAD_DIM
    row2 = lax.broadcasted_iota(jnp.int32, (2 * blk, blk), 0) % blk
    col2 = lax.broadcasted_iota(jnp.int32, (2 * blk, blk), 1)
    causal = col2 < row2

    def split_heads(t, axis):
        zero = jnp.zeros_like(t)
        return jnp.concatenate([jnp.where(even_head, t, zero), jnp.where(even_head, zero, t)],
                               axis=axis)

    def sweep(blocks, pairs, qcats, fresh, between=None):
        n = len(pairs)
        zs = [[_dot(qcats[g], kt_ref[0, j, pairs[g]]) for j, _ in blocks] for g in range(n)]
        if between is not None:
            between()
        log_bs, lcats = [], []
        for g in range(n):
            log_bs.append([])
            lcats.append([])
            for z, (_, diagonal) in zip(zs[g], blocks):
                m = jnp.minimum(z, 0.0)
                soft = jnp.log(1.0 + jnp.exp2(m + m - z)) * LOG2_E
                log_b = m - soft
                log_1m = log_b - z
                if diagonal:
                    log_1m = jnp.where(causal, log_1m, 0.0)
                hi = log_1m.astype(BF16)
                lo = (log_1m - hi.astype(F32)).astype(BF16)
                log_bs[g].append(log_b)
                lcats[g].append(jnp.concatenate([hi, lo], axis=1))
        sums = [[_dot(lcat, tri_scr[...]) for lcat in lcats[g]] for g in range(n)]
        top = None
        acats = []
        for g in range(n):
            carry = None if fresh else carry_scr[g]
            acats.append([])
            for b, (_, diagonal) in enumerate(blocks):
                log_a = log_bs[g][b] + sums[g][b][:, :blk]
                if carry is not None:
                    log_a = log_a + carry
                a = jnp.exp2(log_a)
                if diagonal:
                    a = jnp.where(causal, a, 0.0)
                a = a.astype(BF16)
                acats[g].append(jnp.concatenate([a[:blk], a[blk:]], axis=1))
                total = sums[g][b][:, blk:]
                carry = total if carry is None else carry + total
            carry_scr[g] = carry
            top = carry if top is None else jnp.maximum(top, carry)
        for g in range(n):
            acc = None if fresh else acc_scr[g]
            for b, (j, _) in enumerate(blocks):
                part = _dot(acats[g][b], split_heads(v_ref[0, j, pairs[g]], 0))
                acc = part if acc is None else acc + part
            acc_scr[g] = acc
        return jnp.max(top)

    project_q(0)
    for qi in range(nqb):
        jq = seq_tile * nqb + qi
        for p0 in range(0, npairs, pair_group):
            pairs = tuple(range(p0, p0 + pair_group))
            qcats = [split_heads(q_scr[p, rows_of(qi), :], 0) for p in pairs]

            def neighbours(qi=qi):
                if qi + 1 < nqb:
                    project_q(qi + 1)
                if qi > 0:
                    project_out(qi - 1)

            between = neighbours if p0 == 0 else None
            if qi == 0:
                top = sweep([(jq, True)], pairs, qcats, True, between)
                j_next = jq - 1
            else:
                older = min(qi, 2)
                blocks = [(jq, True)] + [(jq - 1 - i, False) for i in range(older)]
                top = sweep(blocks, pairs, qcats, True, between)
                j_next = jq - 1 - older

            def cond(state):
                j, top = state
                return (j >= 0) & (top >= F32_EXP2_ZERO_BELOW)

            def step(state, pairs=pairs, qcats=qcats):
                j, _ = state
                return j - 1, sweep([(j, False)], pairs, qcats, False)

            lax.while_loop(cond, step, (j_next, top))
            for g, p in enumerate(pairs):
                out_scr[p, rows_of(qi), :] = acc_scr[g].astype(BF16)
    project_out(nqb - 1)


def _attention_layer(x, wq, kt, v, wo, mix_g, mix_b, *, alpha, tm):
    bsz, seq, d = x.shape
    nkb, npairs = seq // ATT_BLOCK, d // LANES
    assert seq % tm == 0 and tm % ATT_BLOCK == 0 and d % LANES == 0
    row = lambda a: a.reshape(1, -1)
    kv_spec = pl.BlockSpec((1, nkb, npairs, ATT_BLOCK, LANES), lambda b, s: (b, 0, 0, 0, 0),
                           pipeline_mode=pl.Buffered(1))
    pair_group = math.gcd(npairs, ATT_PAIR_GROUP)
    scale = HEAD_DIM ** -0.5 * LOG2_E
    body = functools.partial(_attn_body, alpha=alpha, pair_group=pair_group)
    return pl.pallas_call(
        body,
        grid=(bsz, seq // tm),
        in_specs=[_row_tile(tm, d), _resident((d, d)), kv_spec, kv_spec, _resident((d, d)),
                  _resident((1, d)), _resident((1, d))],
        out_specs=_row_tile(tm, d),
        out_shape=jax.ShapeDtypeStruct((bsz, seq, d), F32),
        scratch_shapes=[pltpu.VMEM((npairs, tm, LANES), BF16),
                        pltpu.VMEM((npairs, tm, LANES), BF16),
                        pltpu.VMEM((2 * ATT_BLOCK, 2 * ATT_BLOCK), BF16),
                        pltpu.VMEM((pair_group, 2 * ATT_BLOCK, ATT_BLOCK), F32),
                        pltpu.VMEM((pair_group, ATT_BLOCK, LANES), F32)],
        compiler_params=_params(),
        name="stickbreak_attention",
    )(x, (wq * scale).astype(BF16), kt, v, wo.astype(BF16), row(mix_g), row(mix_b))


def kernel(x, p, a_pw1_w, a_pw1_b, a_dw_w, a_dw_b, a_ln_g, a_ln_b, a_pw2_w, a_pw2_b, b_wq, kv_wk, kv_wv, b_wo, ln_mix_g, ln_mix_b, ffn_w_up, ffn_w_gate, ffn_conv_w, ffn_conv_b, ffn_w_down, ple_w_gate, ple_w_proj, ln_ffn_g, ln_ffn_b):
    depth = p.shape[0]
    n_conf = a_pw1_w.shape[0]
    seq = x.shape[1]
    alpha = (2.0 * depth) ** 0.25
    tm = min(512, seq)
    ffn_tm = 2 * tm if seq % (2 * tm) == 0 else tm
    f = ffn_w_up.shape[-1]
    f_chunk = 256 if f % 256 == 0 else LANES
    kt = v = None
    for i in range(depth):
        if i < n_conf:
            x = _conformer_layer(x, a_pw1_w[i], a_pw1_b[i], a_dw_w[i], a_dw_b[i], a_ln_g[i],
                                 a_ln_b[i], a_pw2_w[i], a_pw2_b[i], ln_mix_g[i], ln_mix_b[i],
                                 alpha=alpha, tm=ffn_tm, n_sub=CONFORMER_SUB_TILES)
        else:
            j = i - n_conf
            if kt is None:
                kt, v = _kv_project(x, kv_wk, kv_wv, tm=tm)
            x = _attention_layer(x, b_wq[j], kt, v, b_wo[j], ln_mix_g[i], ln_mix_b[i],
                                 alpha=alpha, tm=tm)
        x = _ffn_layer(x, p, i, ffn_w_up[i], ffn_w_gate[i], ffn_conv_w[i], ffn_conv_b[i],
                       ffn_w_down[i], ple_w_gate[i], ple_w_proj[i], ln_ffn_g[i], ln_ffn_b[i],
                       alpha=alpha, tm=ffn_tm, n_sub=ffn_tm // tm, f_chunk=f_chunk)
    return x
```

```python
import functools
import math

import jax
import jax.numpy as jnp
from jax import lax
from jax.experimental import pallas as pl
from jax.experimental.pallas import tpu as pltpu

F32 = jnp.float32
BF16 = jnp.bfloat16

LN_EPS = 1e-5
HEAD_DIM = 64
LANES = 128
SUBLANES = 8
ATT_BLOCK = 128
CONV_HALO = 32
LOG2_E = 1.4426950408889634
F32_EXP2_ZERO_BELOW = -151.0
CONFORMER_SUB_TILES = 2
FFN_LOOKAHEAD = 3
ATT_PAIR_GROUP = 8
VMEM_LIMIT_BYTES = 56 * 1024 * 1024


def _layer_norm(v, g, b):
    mu = jnp.mean(v, axis=-1, keepdims=True)
    c = v - mu
    var = jnp.mean(c * c, axis=-1, keepdims=True)
    return c * lax.rsqrt(var + LN_EPS) * g + b


def _dot(a, b):
    return jnp.dot(a, b, preferred_element_type=F32)


def _resident(shape):
    zeros = (0,) * len(shape)
    return pl.BlockSpec(shape, lambda b, s: zeros, pipeline_mode=pl.Buffered(1))


def _row_tile(tm, d):
    return pl.BlockSpec((1, tm, d), lambda b, s: (b, s, 0))


def _params():
    return pltpu.CompilerParams(dimension_semantics=("arbitrary", "arbitrary"),
                                vmem_limit_bytes=VMEM_LIMIT_BYTES)


def _conformer_body(x_ref, pw1_ref, pw1b_ref, dww_ref, dwb_ref, lng_ref, lnb_ref, pw2_ref,
                    pw2b_ref, mg_ref, mb_ref, o_ref, hbuf_ref, cbuf_ref, *, alpha, n_sub):
    tm, d = x_ref.shape[1], x_ref.shape[2]
    nslab, conv_w = dww_ref.shape[0], dww_ref.shape[1]
    sub = tm // n_sub

    @pl.when(pl.program_id(1) == 0)
    def _():
        hbuf_ref[:, 0:CONV_HALO, :] = jnp.zeros((nslab, CONV_HALO, LANES), F32)

    def rows_of(t):
        return slice(t * sub, (t + 1) * sub)

    def pointwise_in(t):
        return _dot(x_ref[0, rows_of(t), :].astype(BF16), pw1_ref[...]) + pw1b_ref[...]

    def glu_to_history(t, h2):
        h = h2[:, :d] * jax.nn.sigmoid(h2[:, d:])
        for s in range(nslab):
            hbuf_ref[s, CONV_HALO + t * sub:CONV_HALO + (t + 1) * sub, :] = \
                h[:, s * LANES:(s + 1) * LANES]

    def conv(t):
        for s in range(nslab):
            taps = [jnp.broadcast_to(dww_ref[s, k:k + 1, :], (SUBLANES, LANES))
                    for k in range(conv_w)]
            bias = jnp.broadcast_to(dwb_ref[s], (SUBLANES, LANES))
            for base, stride in _strided_row_passes(sub):
                for v in range(t * sub + base, t * sub + base + stride):
                    acc = [bias, None]
                    for k in range(conv_w):
                        start = v + (CONV_HALO - (conv_w - 1) + k)
                        term = hbuf_ref[s, pl.ds(start, SUBLANES, stride=stride), :] * taps[k]
                        acc[k % 2] = term if acc[k % 2] is None else acc[k % 2] + term
                    cbuf_ref[s, pl.ds(v, SUBLANES, stride=stride), :] = acc[0] + acc[1]

    def pointwise_out(t):
        c = jnp.concatenate([cbuf_ref[s, rows_of(t), :] for s in range(nslab)], axis=1)
        y = _layer_norm(c, lng_ref[...], lnb_ref[...])
        y = y * jax.nn.sigmoid(y)
        return _dot(y.astype(BF16), pw2_ref[...]) + pw2b_ref[...]

    def finish(t, mix):
        o_ref[0, rows_of(t), :] = _layer_norm(alpha * x_ref[0, rows_of(t), :] + mix,
                                              mg_ref[...], mb_ref[...])

    glu_to_history(0, pointwise_in(0))
    mix = None
    for t in range(n_sub):
        h2_next = pointwise_in(t + 1) if t + 1 < n_sub else None
        conv(t)
        if mix is not None:
            finish(t - 1, mix)
        if h2_next is not None:
            glu_to_history(t + 1, h2_next)
        mix = pointwise_out(t)
    finish(n_sub - 1, mix)
    hbuf_ref[:, 0:CONV_HALO, :] = hbuf_ref[:, tm:tm + CONV_HALO, :]


def _strided_row_passes(n_rows):
    n = n_rows // SUBLANES
    if n % 8 != 0:
        return [(0, n)]
    return [(0, n - 4), (SUBLANES * (n - 4), 4)]


def _conformer_layer(x, pw1_w, pw1_b, dw_w, dw_b, ln_g, ln_b, pw2_w, pw2_b, mix_g, mix_b, *,
                     alpha, tm, n_sub):
    bsz, seq, d = x.shape
    conv_w = dw_w.shape[0]
    assert conv_w - 1 <= CONV_HALO <= tm and seq % tm == 0 and d % LANES == 0
    assert tm % (n_sub * SUBLANES) == 0
    nslab = d // LANES
    row = lambda v: v.reshape(1, -1)
    slabs = lambda v: v.reshape(-1, nslab, LANES).transpose(1, 0, 2)
    body = functools.partial(_conformer_body, alpha=alpha, n_sub=n_sub)
    return pl.pallas_call(
        body,
        grid=(bsz, seq // tm),
        in_specs=[_row_tile(tm, d), _resident((d, 2 * d)), _resident((1, 2 * d)),
                  _resident((nslab, conv_w, LANES)), _resident((nslab, 1, LANES)),
                  _resident((1, d)), _resident((1, d)),
                  _resident((d, d)), _resident((1, d)), _resident((1, d)), _resident((1, d))],
        out_specs=_row_tile(tm, d),
        out_shape=jax.ShapeDtypeStruct((bsz, seq, d), F32),
        scratch_shapes=[pltpu.VMEM((nslab, CONV_HALO + tm, LANES), F32),
                        pltpu.VMEM((nslab, tm, LANES), F32)],
        compiler_params=_params(),
        name="conformer_mixer",
    )(x, pw1_w.astype(BF16), row(pw1_b), slabs(dw_w), slabs(dw_b), row(ln_g), row(ln_b),
      pw2_w.astype(BF16), row(pw2_b), row(mix_g), row(mix_b))


def _ffn_body(x_ref, p_ref, wup_ref, wgate_ref, cw_ref, cb_ref, wdown_ref, pg_ref, pp_ref,
              g_ref, b_ref, o_ref, gprev_ref, gbuf_ref, acc_ref, ple_ref, *, alpha, f_chunk):
    n_sub, sub = acc_ref.shape[0], acc_ref.shape[1]
    f = wup_ref.shape[1]
    conv_w = cw_ref.shape[0]
    nchunks = f // f_chunk

    @pl.when(pl.program_id(1) == 0)
    def _():
        gprev_ref[...] = jnp.zeros(gprev_ref.shape, F32)

    n_slices = min(nchunks - 1, 8) if sub % 64 == 0 else 1
    slice_rows = sub // n_slices

    def finish_rows(t, i):
        rs = slice(i * slice_rows, (i + 1) * slice_rows)
        out_rows = slice(t * sub + i * slice_rows, t * sub + (i + 1) * slice_rows)
        gate = jax.nn.sigmoid(ple_ref[0, rs, :])
        y = alpha * x_ref[0, out_rows, :] + acc_ref[t, rs, :] + gate * ple_ref[1, rs, :]
        o_ref[0, out_rows, :] = _layer_norm(y, g_ref[...], b_ref[...])

    for t in range(n_sub):
        rows = slice(t * sub, (t + 1) * sub)
        xb = x_ref[0, rows, :].astype(BF16)

        def up_gate(c, xb=xb):
            fs = slice(c * f_chunk, (c + 1) * f_chunk)
            return _dot(xb, wup_ref[:, fs]), _dot(xb, wgate_ref[:, fs])

        queue = [up_gate(c) for c in range(min(FFN_LOOKAHEAD, nchunks))]
        for c in range(nchunks):
            fs = slice(c * f_chunk, (c + 1) * f_chunk)
            if c + FFN_LOOKAHEAD < nchunks:
                queue.append(up_gate(c + FFN_LOOKAHEAD))
            u, g = queue.pop(0)
            if t > 0 and c < n_slices:
                finish_rows(t - 1, c)
            if c + 1 == nchunks:
                ple_ref[0] = _dot(xb, pg_ref[...])
                ple_ref[1] = _dot(p_ref[0, 0, rows, :].astype(BF16), pp_ref[...])
            gbuf_ref[0:SUBLANES, :] = gprev_ref[:, fs]
            gbuf_ref[SUBLANES:SUBLANES + sub, :] = g
            gprev_ref[:, fs] = g[sub - SUBLANES:sub, :]
            gc = cb_ref[:, fs] + cw_ref[conv_w - 1:conv_w, fs] * g
            for k in range(conv_w - 1):
                off = SUBLANES - (conv_w - 1) + k
                gc = gc + cw_ref[k:k + 1, fs] * gbuf_ref[off:off + sub, :]
            hidden = (gc * jax.nn.sigmoid(gc) * u).astype(BF16)
            part = _dot(hidden, wdown_ref[fs, :])
            if c == 0:
                acc_ref[t] = part
            else:
                acc_ref[t] += part
    for i in range(n_slices):
        finish_rows(n_sub - 1, i)


def _ffn_layer(x, p, layer, w_up, w_gate, conv_w, conv_b, w_down, ple_gate, ple_proj, ln_g,
               ln_b, *, alpha, tm, n_sub, f_chunk):
    bsz, seq, d = x.shape
    f = w_up.shape[1]
    pd = p.shape[-1]
    kw = conv_w.shape[0]
    sub = tm // n_sub
    assert kw - 1 <= SUBLANES <= sub and seq % tm == 0 and tm % n_sub == 0 and f % f_chunk == 0
    row = lambda v: v.reshape(1, -1)
    body = functools.partial(_ffn_body, alpha=alpha, f_chunk=f_chunk)
    p_spec = pl.BlockSpec((1, 1, tm, pd), lambda b, s: (layer, b, s, 0))
    return pl.pallas_call(
        body,
        grid=(bsz, seq // tm),
        in_specs=[_row_tile(tm, d), p_spec, _resident((d, f)), _resident((d, f)),
                  _resident((kw, f)), _resident((1, f)), _resident((f, d)), _resident((d, d)),
                  _resident((pd, d)), _resident((1, d)), _resident((1, d))],
        out_specs=_row_tile(tm, d),
        out_shape=jax.ShapeDtypeStruct((bsz, seq, d), F32),
        scratch_shapes=[pltpu.VMEM((SUBLANES, f), F32), pltpu.VMEM((SUBLANES + sub, f_chunk), F32),
                        pltpu.VMEM((n_sub, sub, d), F32), pltpu.VMEM((2, sub, d), F32)],
        compiler_params=_params(),
        name="ffn_ple",
    )(x, p, w_up.astype(BF16), w_gate.astype(BF16), conv_w, row(conv_b), w_down.astype(BF16),
      ple_gate.astype(BF16), ple_proj.astype(BF16), row(ln_g), row(ln_b))


def _kv_body(x_ref, wkt_ref, wv_ref, kt_ref, v_ref):
    tm, d = x_ref.shape[1], x_ref.shape[2]
    xb = x_ref[0].astype(BF16)
    kt = lax.dot_general(wkt_ref[...], xb, (((1,), (1,)), ((), ())),
                         preferred_element_type=F32)
    v = _dot(xb, wv_ref[...])
    for j in range(tm // ATT_BLOCK):
        js = slice(j * ATT_BLOCK, (j + 1) * ATT_BLOCK)
        for p in range(d // LANES):
            ps = slice(p * LANES, (p + 1) * LANES)
            kt_ref[0, j, p] = kt[ps, js].astype(BF16)
            v_ref[0, j, p] = v[js, ps].astype(BF16)


def _kv_project(x, wk, wv, *, tm):
    bsz, seq, d = x.shape
    nkb, npairs = seq // ATT_BLOCK, d // LANES
    blocked = jax.ShapeDtypeStruct((bsz, nkb, npairs, ATT_BLOCK, LANES), BF16)
    spec = pl.BlockSpec((1, tm // ATT_BLOCK, npairs, ATT_BLOCK, LANES),
                        lambda b, s: (b, s, 0, 0, 0))
    return pl.pallas_call(
        _kv_body,
        grid=(bsz, seq // tm),
        in_specs=[_row_tile(tm, d), _resident((d, d)), _resident((d, d))],
        out_specs=[spec, spec],
        out_shape=[blocked, blocked],
        compiler_params=_params(),
        name="kv_project",
    )(x, wk.T.astype(BF16), wv.astype(BF16))


def _attn_body(x_ref, wq_ref, kt_ref, v_ref, wo_ref, mg_ref, mb_ref, o_ref,
               q_scr, out_scr, tri_scr, carry_scr, acc_scr, *, alpha, pair_group):
    tm, d = x_ref.shape[1], x_ref.shape[2]
    npairs = d // LANES
    nqb = tm // ATT_BLOCK
    blk = ATT_BLOCK
    seq_tile = pl.program_id(1)

    def rows_of(qi):
        return slice(qi * blk, (qi + 1) * blk)

    def project_q(qi):
        q = _dot(x_ref[0, rows_of(qi), :].astype(BF16), wq_ref[...])
        for p in range(npairs):
            q_scr[p, rows_of(qi), :] = q[:, p * LANES:(p + 1) * LANES].astype(BF16)

    def project_out(qi):
        attn = jnp.concatenate([out_scr[p, rows_of(qi), :] for p in range(npairs)], axis=1)
        mix = _dot(attn, wo_ref[...])
        o_ref[0, rows_of(qi), :] = _layer_norm(alpha * x_ref[0, rows_of(qi), :] + mix,
                                               mg_ref[...], mb_ref[...])

    rr = lax.broadcasted_iota(jnp.int32, (2 * blk, 2 * blk), 0) % blk
    cc = lax.broadcasted_iota(jnp.int32, (2 * blk, 2 * blk), 1)
    tri_scr[...] = jnp.where((cc >= blk) | (rr > cc), 1.0, 0.0).astype(BF16)

    lane = lax.broadcasted_iota(jnp.int32, (blk, LANES), 1)
    even_head = lane < HEAD_DIM
    row2 = lax.broadcasted_iota(jnp.int32, (2 * blk, blk), 0) % blk
    col2 = lax.broadcasted_iota(jnp.int32, (2 * blk, blk), 1)
    causal = col2 < row2

    def split_heads(t, axis):
        zero = jnp.zeros_like(t)
        return jnp.concatenate([jnp.where(even_head, t, zero), jnp.where(even_head, zero, t)],
                               axis=axis)

    def sweep(blocks, pairs, qcats, fresh, between=None):
        n = len(pairs)
        zs = [[_dot(qcats[g], kt_ref[0, j, pairs[g]]) for j, _ in blocks] for g in range(n)]
        if between is not None:
            between()
        log_bs, lcats = [], []
        for g in range(n):
            log_bs.append([])
            lcats.append([])
            for z, (_, diagonal) in zip(zs[g], blocks):
                m = jnp.minimum(z, 0.0)
                soft = jnp.log(1.0 + jnp.exp2(m + m - z)) * LOG2_E
                log_b = m - soft
                log_1m = log_b - z
                if diagonal:
                    log_1m = jnp.where(causal, log_1m, 0.0)
                hi = log_1m.astype(BF16)
                lo = (log_1m - hi.astype(F32)).astype(BF16)
                log_bs[g].append(log_b)
                lcats[g].append(jnp.concatenate([hi, lo], axis=1))
        sums = [[_dot(lcat, tri_scr[...]) for lcat in lcats[g]] for g in range(n)]
        top = None
        acats = []
        for g in range(n):
            carry = None if fresh else carry_scr[g]
            acats.append([])
            for b, (_, diagonal) in enumerate(blocks):
                log_a = log_bs[g][b] + sums[g][b][:, :blk]
                if carry is not None:
                    log_a = log_a + carry
                a = jnp.exp2(log_a)
                if diagonal:
                    a = jnp.where(causal, a, 0.0)
                a = a.astype(BF16)
                acats[g].append(jnp.concatenate([a[:blk], a[blk:]], axis=1))
                total = sums[g][b][:, blk:]
                carry = total if carry is None else carry + total
            carry_scr[g] = carry
            top = carry if top is None else jnp.maximum(top, carry)
        for g in range(n):
            acc = None if fresh else acc_scr[g]
            for b, (j, _) in enumerate(blocks):
                part = _dot(acats[g][b], split_heads(v_ref[0, j, pairs[g]], 0))
                acc = part if acc is None else acc + part
            acc_scr[g] = acc
        return jnp.max(top)

    project_q(0)
    for qi in range(nqb):
        jq = seq_tile * nqb + qi
        for p0 in range(0, npairs, pair_group):
            pairs = tuple(range(p0, p0 + pair_group))
            qcats = [split_heads(q_scr[p, rows_of(qi), :], 0) for p in pairs]

            def neighbours(qi=qi):
                if qi + 1 < nqb:
                    project_q(qi + 1)
                if qi > 0:
                    project_out(qi - 1)

            between = neighbours if p0 == 0 else None
            if qi == 0:
                top = sweep([(jq, True)], pairs, qcats, True, between)
                j_next = jq - 1
            else:
                older = min(qi, 2)
                blocks = [(jq, True)] + [(jq - 1 - i, False) for i in range(older)]
                top = sweep(blocks, pairs, qcats, True, between)
                j_next = jq - 1 - older

            def cond(state):
                j, top = state
                return (j >= 0) & (top >= F32_EXP2_ZERO_BELOW)

            def step(state, pairs=pairs, qcats=qcats):
                j, _ = state
                return j - 1, sweep([(j, False)], pairs, qcats, False)

            lax.while_loop(cond, step, (j_next, top))
            for g, p in enumerate(pairs):
                out_scr[p, rows_of(qi), :] = acc_scr[g].astype(BF16)
    project_out(nqb - 1)


def _attention_layer(x, wq, kt, v, wo, mix_g, mix_b, *, alpha, tm):
    bsz, seq, d = x.shape
    nkb, npairs = seq // ATT_BLOCK, d // LANES
    assert seq % tm == 0 and tm % ATT_BLOCK == 0 and d % LANES == 0
    row = lambda a: a.reshape(1, -1)
    kv_spec = pl.BlockSpec((1, nkb, npairs, ATT_BLOCK, LANES), lambda b, s: (b, 0, 0, 0, 0),
                           pipeline_mode=pl.Buffered(1))
    pair_group = math.gcd(npairs, ATT_PAIR_GROUP)
    scale = HEAD_DIM ** -0.5 * LOG2_E
    body = functools.partial(_attn_body, alpha=alpha, pair_group=pair_group)
    return pl.pallas_call(
        body,
        grid=(bsz, seq // tm),
        in_specs=[_row_tile(tm, d), _resident((d, d)), kv_spec, kv_spec, _resident((d, d)),
                  _resident((1, d)), _resident((1, d))],
        out_specs=_row_tile(tm, d),
        out_shape=jax.ShapeDtypeStruct((bsz, seq, d), F32),
        scratch_shapes=[pltpu.VMEM((npairs, tm, LANES), BF16),
                        pltpu.VMEM((npairs, tm, LANES), BF16),
                        pltpu.VMEM((2 * ATT_BLOCK, 2 * ATT_BLOCK), BF16),
                        pltpu.VMEM((pair_group, 2 * ATT_BLOCK, ATT_BLOCK), F32),
                        pltpu.VMEM((pair_group, ATT_BLOCK, LANES), F32)],
        compiler_params=_params(),
        name="stickbreak_attention",
    )(x, (wq * scale).astype(BF16), kt, v, wo.astype(BF16), row(mix_g), row(mix_b))


def kernel(x, p, a_pw1_w, a_pw1_b, a_dw_w, a_dw_b, a_ln_g, a_ln_b, a_pw2_w, a_pw2_b, b_wq, kv_wk, kv_wv, b_wo, ln_mix_g, ln_mix_b, ffn_w_up, ffn_w_gate, ffn_conv_w, ffn_conv_b, ffn_w_down, ple_w_gate, ple_w_proj, ln_ffn_g, ln_ffn_b):
    depth = p.shape[0]
    n_conf = a_pw1_w.shape[0]
    seq = x.shape[1]
    alpha = (2.0 * depth) ** 0.25
    tm = min(512, seq)
    ffn_tm = 2 * tm if seq % (2 * tm) == 0 else tm
    f = ffn_w_up.shape[-1]
    f_chunk = 256 if f % 256 == 0 else LANES
    kt = v = None
    for i in range(depth):
        if i < n_conf:
            x = _conformer_layer(x, a_pw1_w[i], a_pw1_b[i], a_dw_w[i], a_dw_b[i], a_ln_g[i],
                                 a_ln_b[i], a_pw2_w[i], a_pw2_b[i], ln_mix_g[i], ln_mix_b[i],
                                 alpha=alpha, tm=ffn_tm, n_sub=CONFORMER_SUB_TILES)
        else:
            j = i - n_conf
            if kt is None:
                kt, v = _kv_project(x, kv_wk, kv_wv, tm=tm)
            x = _attention_layer(x, b_wq[j], kt, v, b_wo[j], ln_mix_g[i], ln_mix_b[i],
                                 alpha=alpha, tm=ffn_tm)
        x = _ffn_layer(x, p, i, ffn_w_up[i], ffn_w_gate[i], ffn_conv_w[i], ffn_conv_b[i],
                       ffn_w_down[i], ple_w_gate[i], ple_w_proj[i], ln_ffn_g[i], ln_ffn_b[i],
                       alpha=alpha, tm=ffn_tm, n_sub=ffn_tm // tm, f_chunk=f_chunk)
    return x
```
